```python
import math
import jax, jax.numpy as jnp
from jax import lax
import numpy as np

D_MODEL = 2048
BATCH = 4
SEQ = 8192
DEPTH = 1
DEC_BATCH = 2
DEC_SEQ = 8192
PAST_LEN = 128

HEAD_DIM = 64
N_ATTN_HEADS = (3 * D_MODEL // 4) // HEAD_DIM
ATTN_WIDTH = N_ATTN_HEADS * HEAD_DIM
FOURIER_WIDTH = D_MODEL - ATTN_WIDTH
N_FOURIER_GROUPS = 4
FOURIER_GROUP = FOURIER_WIDTH // N_FOURIER_GROUPS
IN_PROJ_WIDTH = 3 * ATTN_WIDTH + FOURIER_WIDTH
DILATION_PATTERNS = ((128, 1), (512, 4), (2048, 16))
ROPE_THETA = 500000.0
ROPE_DIM = HEAD_DIM // 4
D_FF = 5632
CONV_WIDTH = 3
EPS = 1e-6
MASK_VALUE = -1e30

kernel_name = "hybrid_dilated_attn_fnet_convffn_encoder"


def rms_norm(x, g):
    xf = x.astype(jnp.float32)
    var = jnp.mean(xf * xf, axis=-1, keepdims=True)
    return (xf * lax.rsqrt(var + EPS) * g.astype(jnp.float32)).astype(x.dtype)


def rope_tables(seq_len):
    inv_freq = ROPE_THETA ** (-jnp.arange(0, ROPE_DIM, 2, dtype=jnp.float32) / ROPE_DIM)
    ang = jnp.arange(seq_len, dtype=jnp.float32)[:, None] * inv_freq[None, :]
    return jnp.cos(ang), jnp.sin(ang)


def partial_rotary(x, cos, sin):
    half = ROPE_DIM // 2
    x1 = x[..., :half]
    x2 = x[..., half:ROPE_DIM]
    c = cos[None, :, None, :]
    s = sin[None, :, None, :]
    return jnp.concatenate([x1 * c - x2 * s, x2 * c + x1 * s, x[..., ROPE_DIM:]], axis=-1)


def dilated_window_attention(q, k, v, window, dilation):
    B, S, H, Dh = q.shape
    d = dilation
    R = window // (2 * d)
    T = S // d
    G = B * d

    def to_classes(a):
        return a.reshape(B, T, d, H, Dh).transpose(0, 2, 1, 3, 4).reshape(G, T, H, Dh)

    qc, kc, vc = to_classes(q), to_classes(k), to_classes(v)
    nb = -(-T // R)
    Tp = nb * R
    qb = jnp.pad(qc, ((0, 0), (0, Tp - T), (0, 0), (0, 0))).reshape(G, nb, R, H, Dh)

    def key_windows(a):
        ap = jnp.pad(a, ((0, 0), (R, Tp - T + R), (0, 0), (0, 0))).reshape(G, nb + 2, R, H, Dh)
        return jnp.concatenate([ap[:, :-2], ap[:, 1:-1], ap[:, 2:]], axis=2)

    kw, vw = key_windows(kc), key_windows(vc)
    t_q = (jnp.arange(nb) * R)[:, None] + jnp.arange(R)[None, :]
    t_k = (jnp.arange(nb) * R - R)[:, None] + jnp.arange(3 * R)[None, :]
    diff = t_k[:, None, :] - t_q[:, :, None]
    valid = (jnp.abs(diff) <= R) & (t_k[:, None, :] >= 0) & (t_k[:, None, :] < T)

    scale = 1.0 / math.sqrt(Dh)
    s = jnp.einsum('gnqhd,gnkhd->gnhqk', qb, kw) * scale
    s = jnp.where(valid[None, :, None, :, :], s, MASK_VALUE)
    m = jnp.max(s, axis=-1, keepdims=True)
    p = jnp.exp(s - m)
    den = jnp.sum(p, axis=-1, keepdims=True)
    o = jnp.einsum('gnhqk,gnkhd->gnqhd', p / den, vw)
    lse = (m + jnp.log(den))[..., 0]

    o = o.reshape(G, Tp, H, Dh)[:, :T].reshape(B, d, T, H, Dh)
    o = o.transpose(0, 2, 1, 3, 4).reshape(B, S, H, Dh)
    lse = lse.transpose(0, 1, 3, 2).reshape(G, Tp, H)[:, :T].reshape(B, d, T, H)
    lse = lse.transpose(0, 2, 1, 3).reshape(B, S, H)
    return o, lse


def dilated_mixture_attention(q, k, v):
    outs, lses = [], []
    for window, dilation in DILATION_PATTERNS:
        o, l = dilated_window_attention(q, k, v, window, dilation)
        outs.append(o)
        lses.append(l)
    w = jax.nn.softmax(jnp.stack(lses, axis=0), axis=0)
    out = w[0][..., None] * outs[0]
    for i in range(1, len(outs)):
        out = out + w[i][..., None] * outs[i]
    return out


def fourier_mix(f):
    B, S, _ = f.shape
    fg = f.astype(jnp.float32).reshape(B, S, N_FOURIER_GROUPS, FOURIER_GROUP)
    out = jnp.fft.fft2(fg, axes=(1, 3), norm='ortho').real
    return out.reshape(B, S, FOURIER_WIDTH).astype(f.dtype)


def centred_depthwise_conv(u, w, b):
    S = u.shape[1]
    pad = CONV_WIDTH // 2
    up = jnp.pad(u, ((0, 0), (pad, pad), (0, 0)))
    out = up[:, 0:S] * w[0]
    for j in range(1, CONV_WIDTH):
        out = out + up[:, j:j + S] * w[j]
    return out + b


def encoder_layer(x, norm1_g, w_in, attn_out_g, fourier_out_g, w_out,
                  norm2_g, w_up, conv_w, conv_b, w_down):
    B, S, _ = x.shape
    h = rms_norm(x, norm1_g)
    proj = h @ w_in
    q = proj[..., :ATTN_WIDTH]
    k = proj[..., ATTN_WIDTH:2 * ATTN_WIDTH]
    v = proj[..., 2 * ATTN_WIDTH:3 * ATTN_WIDTH]
    f = proj[..., 3 * ATTN_WIDTH:]

    cos, sin = rope_tables(S)
    shp = (B, S, N_ATTN_HEADS, HEAD_DIM)
    q = partial_rotary(q.astype(jnp.float32).reshape(shp), cos, sin)
    k = partial_rotary(k.astype(jnp.float32).reshape(shp), cos, sin)
    v = v.astype(jnp.float32).reshape(shp)
    attn = dilated_mixture_attention(q, k, v).reshape(B, S, ATTN_WIDTH).astype(x.dtype)

    four = fourier_mix(f)
    merged = jnp.concatenate([rms_norm(attn, attn_out_g), rms_norm(four, fourier_out_g)], axis=-1)
    x = x + merged @ w_out

    h2 = rms_norm(x, norm2_g)
    u = centred_depthwise_conv(h2 @ w_up, conv_w, conv_b)
    gate = u[..., :D_FF]
    val = u[..., D_FF:]
    x = x + (jax.nn.silu(gate) * val) @ w_down
    return x


def encoder(x, norm1_g, w_in, attn_out_g, fourier_out_g, w_out,
            norm2_g, w_up, conv_w, conv_b, w_down, final_g):
    for l in range(DEPTH):
        x = encoder_layer(x, norm1_g[l], w_in[l], attn_out_g[l], fourier_out_g[l], w_out[l],
                          norm2_g[l], w_up[l], conv_w[l], conv_b[l], w_down[l])
    return rms_norm(x, final_g)


def setup_inputs(seed: int = 0) -> dict:
    key = jax.random.key(seed)
    ks = jax.random.split(key, 14)
    f32 = jnp.float32
    conv_base = jnp.array([0.25, 0.5, 0.25], dtype=f32)[None, :, None]
    return {
        'x_prompt': jax.random.normal(ks[0], (BATCH, SEQ, D_MODEL), f32),
        'x_sample': jax.random.normal(ks[1], (DEC_BATCH, DEC_SEQ, D_MODEL), f32),
        'norm1_g': 1.0 + 0.02 * jax.random.normal(ks[2], (DEPTH, D_MODEL), f32),
        'w_in': jax.random.normal(ks[3], (DEPTH, D_MODEL, IN_PROJ_WIDTH), f32) * D_MODEL ** -0.5,
        'attn_out_g': 1.0 + 0.02 * jax.random.normal(ks[4], (DEPTH, ATTN_WIDTH), f32),
        'fourier_out_g': 1.0 + 0.02 * jax.random.normal(ks[5], (DEPTH, FOURIER_WIDTH), f32),
        'w_out': jax.random.normal(ks[6], (DEPTH, D_MODEL, D_MODEL), f32) * D_MODEL ** -0.5,
        'norm2_g': 1.0 + 0.02 * jax.random.normal(ks[7], (DEPTH, D_MODEL), f32),
        'w_up': jax.random.normal(ks[8], (DEPTH, D_MODEL, 2 * D_FF), f32) * D_MODEL ** -0.5,
        'conv_w': conv_base + 0.1 * jax.random.normal(ks[9], (DEPTH, CONV_WIDTH, 2 * D_FF), f32),
        'conv_b': 0.01 * jax.random.normal(ks[10], (DEPTH, 2 * D_FF), f32),
        'w_down': jax.random.normal(ks[11], (DEPTH, D_FF, D_MODEL), f32) * D_FF ** -0.5,
        'final_g': 1.0 + 0.02 * jax.random.normal(ks[12], (D_MODEL,), f32),
    }


def reference(x_prompt, x_sample, norm1_g, w_in, attn_out_g, fourier_out_g, w_out,
              norm2_g, w_up, conv_w, conv_b, w_down, final_g):
    y_prompt = encoder(x_prompt, norm1_g, w_in, attn_out_g, fourier_out_g, w_out,
                       norm2_g, w_up, conv_w, conv_b, w_down, final_g)
    y_sample = encoder(x_sample, norm1_g, w_in, attn_out_g, fourier_out_g, w_out,
                       norm2_g, w_up, conv_w, conv_b, w_down, final_g)
    return (y_prompt, y_sample)
```

```python
import functools
import math

import numpy as np
import jax
import jax.numpy as jnp
from jax import lax
from jax.experimental import pallas as pl
from jax.experimental.pallas import tpu as pltpu

F32 = jnp.float32
BF16 = jnp.bfloat16

HEAD_DIM = 64
ROPE_DIM = 16
ROPE_THETA = 500000.0
N_FOURIER_GROUPS = 4
FOURIER_GROUP = 128
DILATION_PATTERNS = ((128, 1), (512, 4), (2048, 16))
BAND = 64
EPS = 1e-6
MASK_VALUE = -1e30

LANES = 128
VMEM_LIMIT_BYTES = 56 * 1024 * 1024

DFT_N1 = 64


def _cparams(semantics):
    return pltpu.CompilerParams(dimension_semantics=semantics,
                                vmem_limit_bytes=VMEM_LIMIT_BYTES)


def _rms(x, g):
    var = jnp.mean(x * x, axis=-1, keepdims=True)
    return x * lax.rsqrt(var + EPS) * g


def _in_proj_kernel(x_ref, g_ref, w_ref, cos_ref, sa_ref, sb_ref, o_ref, f_ref, h_ref,
                    *, n_rope_tiles, n_attn_tiles):
    j = pl.program_id(1)

    @pl.when(j == 0)
    def _():
        h_ref[...] = _rms(x_ref[...], g_ref[...]).astype(BF16)

    acc = jnp.dot(h_ref[...], w_ref[...], preferred_element_type=F32)
    n_chunks = acc.shape[1] // LANES

    @pl.when(j < n_rope_tiles)
    def _():
        cos = cos_ref[...]
        sa = sa_ref[...]
        sb = sb_ref[...]
        for c in range(n_chunks):
            a = acc[:, c * LANES:(c + 1) * LANES]
            r = a * cos + pltpu.roll(a, LANES - ROPE_DIM // 2, 1) * sa \
                + pltpu.roll(a, ROPE_DIM // 2, 1) * sb
            o_ref[:, c * LANES:(c + 1) * LANES] = r.astype(BF16)

    @pl.when((j >= n_rope_tiles) & (j < n_attn_tiles))
    def _():
        o_ref[...] = acc.astype(BF16)

    @pl.when(j == n_attn_tiles)
    def _():
        for c in range(n_chunks):
            f_ref[c] = acc[:, c * LANES:(c + 1) * LANES].astype(BF16)


def _in_proj(x2, g, w, cos_t, sa_t, sb_t, *, batch, seq, attn_width, tm=512, tn=512):
    ntok, d_model = x2.shape
    width = w.shape[1]
    n_tiles = width // tn
    n_attn_tiles = 3 * attn_width // tn
    n_rope_tiles = 2 * attn_width // tn
    assert n_tiles == n_attn_tiles + 1 and tn == N_FOURIER_GROUPS * FOURIER_GROUP
    tps = seq // tm
    kern = functools.partial(_in_proj_kernel, n_rope_tiles=n_rope_tiles,
                             n_attn_tiles=n_attn_tiles)
    tab_spec = pl.BlockSpec((tm, LANES), lambda i, j: (i % tps, 0))
    return pl.pallas_call(
        kern,
        grid=(ntok // tm, n_tiles),
        in_specs=[
            pl.BlockSpec((tm, d_model), lambda i, j: (i, 0)),
            pl.BlockSpec((1, d_model), lambda i, j: (0, 0)),
            pl.BlockSpec((d_model, tn), lambda i, j: (0, j)),
            tab_spec, tab_spec, tab_spec,
        ],
        out_specs=[
            pl.BlockSpec((tm, tn), lambda i, j: (i, jnp.minimum(j, n_attn_tiles - 1))),
            pl.BlockSpec((None, N_FOURIER_GROUPS, tm, FOURIER_GROUP),
                         lambda i, j: (i // tps, 0, i % tps, 0)),
        ],
        out_shape=[
            jax.ShapeDtypeStruct((ntok, 3 * attn_width), BF16),
            jax.ShapeDtypeStruct((batch, N_FOURIER_GROUPS, seq, FOURIER_GROUP), BF16),
        ],
        scratch_shapes=[pltpu.VMEM((tm, d_model), BF16)],
        compiler_params=_cparams(("parallel", "arbitrary")),
        name="in_proj",
    )(x2, g, w, cos_t, sa_t, sb_t)


def _attn_kernel(q_ref, kp_ref, kc_ref, kn_ref, vp_ref, vc_ref, vn_ref,
                 o_ref, lse_ref, kw_ref, vw_ref, *, tb, tq, t_len, pairs):
    j = pl.program_id(2)
    g = pl.program_id(3)
    tk = tq + 2 * BAND

    kw_ref[0:BAND] = kp_ref[...]
    kw_ref[BAND:BAND + tb] = kc_ref[...]
    kw_ref[BAND + tb:BAND + tb + BAND] = kn_ref[...]
    vw_ref[0:BAND] = vp_ref[...]
    vw_ref[BAND:BAND + tb] = vc_ref[...]
    vw_ref[BAND + tb:BAND + tb + BAND] = vn_ref[...]

    @pl.when(g == 0)
    def _():
        lse_ref[...] = jnp.zeros(lse_ref.shape, F32)

    lane = lax.broadcasted_iota(jnp.int32, (tq, LANES), 1)
    first_head = lane < HEAD_DIM

    def body(i, carry):
        r0 = pl.multiple_of(i * tq, tq)
        col = lax.broadcasted_iota(jnp.int32, (tq, tk), 1)
        row = lax.broadcasted_iota(jnp.int32, (tq, tk), 0)
        delta = col - row
        kidx = j * tb + i * tq - BAND + col
        valid = (delta >= 0) & (delta <= 2 * BAND) & (kidx >= 0) & (kidx < t_len)
        lse_rows = lse_ref[pl.ds(r0, tq), :]
        for p in range(pairs):
            lanes = slice(p * LANES, (p + 1) * LANES)
            q = q_ref[pl.ds(r0, tq), lanes]
            kw = kw_ref[pl.ds(r0, tk), lanes]
            vw = vw_ref[pl.ds(r0, tk), lanes]
            outs = []
            for h in range(2):
                keep = first_head if h == 0 else jnp.logical_not(first_head)
                qh = jnp.where(keep, q, jnp.zeros_like(q))
                s = lax.dot_general(qh, kw, (((1,), (1,)), ((), ())),
                                    preferred_element_type=F32)
                s = jnp.where(valid, s, MASK_VALUE)
                m = jnp.max(s, axis=-1, keepdims=True)
                e = jnp.exp(s - m)
                den = jnp.sum(e, axis=-1, keepdims=True)
                pv = jnp.dot(e.astype(BF16), vw, preferred_element_type=F32)
                outs.append(pv * (1.0 / den))
                lse_h = m + jnp.log(den)
                head_lane = 2 * pairs * g + 2 * p + h
                lse_rows = jnp.where(lane == head_lane, lse_h, lse_rows)
            o = jnp.where(first_head, outs[0], outs[1])
            o_ref[pl.ds(r0, tq), lanes] = o.astype(BF16)
        lse_ref[pl.ds(r0, tq), :] = lse_rows
        return carry

    lax.fori_loop(0, tb // tq, body, 0)


def _attention(proj3, *, dilation, attn_width, pairs=2, tq=128):
    batch, seq, width = proj3.shape
    d = dilation
    t_len = seq // d
    tb = min(512, t_len)
    assert t_len % tb == 0 and tb % tq == 0 and tb % BAND == 0
    bw = pairs * LANES
    groups = attn_width // bw
    wblk = width // bw
    hb = tb // BAND
    last_hb = t_len // BAND - 1
    view = proj3.reshape(batch, t_len, d * width)

    def qmap(off):
        return lambda b, r, j, g: (b, j, r * wblk + off * groups + g)

    def prev_map(off):
        return lambda b, r, j, g: (b, jnp.maximum(j * hb - 1, 0), r * wblk + off * groups + g)

    def next_map(off):
        return lambda b, r, j, g: (b, jnp.minimum((j + 1) * hb, last_hb),
                                   r * wblk + off * groups + g)

    cur = lambda off: pl.BlockSpec((None, tb, bw), qmap(off))
    prv = lambda off: pl.BlockSpec((None, BAND, bw), prev_map(off))
    nxt = lambda off: pl.BlockSpec((None, BAND, bw), next_map(off))

    kern = functools.partial(_attn_kernel, tb=tb, tq=tq, t_len=t_len, pairs=pairs)
    o, lse = pl.pallas_call(
        kern,
        grid=(batch, d, t_len // tb, groups),
        in_specs=[cur(0), prv(1), cur(1), nxt(1), prv(2), cur(2), nxt(2)],
        out_specs=[
            pl.BlockSpec((None, tb, bw), lambda b, r, j, g: (b, j, r * groups + g)),
            pl.BlockSpec((None, tb, LANES), lambda b, r, j, g: (b, j, r)),
        ],
        out_shape=[
            jax.ShapeDtypeStruct((batch, t_len, d * attn_width), BF16),
            jax.ShapeDtypeStruct((batch, t_len, d * LANES), F32),
        ],
        scratch_shapes=[pltpu.VMEM((tb + 2 * BAND, bw), BF16),
                        pltpu.VMEM((tb + 2 * BAND, bw), BF16)],
        compiler_params=_cparams(("parallel", "parallel", "parallel", "arbitrary")),
        name=f"attn_d{d}",
    )(view, view, view, view, view, view, view)
    return o.reshape(batch, seq, attn_width), lse.reshape(batch, seq, LANES)


def _dft_stage1_kernel(x_ref, m1_ref, tc_ref, ts_ref, y_ref):
    y = jnp.dot(m1_ref[...], x_ref[...], preferred_element_type=F32)
    yr = y[:DFT_N1]
    yi = y[DFT_N1:]
    tc = tc_ref[...]
    ts = ts_ref[...]
    y_ref[0] = (yr * tc + yi * ts).astype(BF16)
    y_ref[1] = (yi * tc - yr * ts).astype(BF16)


def _dft_stage2_kernel(y_ref, m2_ref, cc_ref, sc_ref, o_ref, *, kb, n2, scale):
    m2 = m2_ref[...]
    cc = cc_ref[...]
    sc = sc_ref[...]
    width = N_FOURIER_GROUPS * FOURIER_GROUP
    for kk in range(kb):
        for gi in range(N_FOURIER_GROUPS):
            dat = jnp.concatenate([y_ref[gi, 0, kk], y_ref[gi, 1, kk]], axis=0)
            o = jnp.dot(m2, dat, preferred_element_type=F32)
            o_re = o[:n2].astype(BF16)
            o_im = o[n2:].astype(BF16)
            res = jnp.dot(o_re, cc, preferred_element_type=F32) \
                + jnp.dot(o_im, sc, preferred_element_type=F32)
            lo = kk * width + gi * FOURIER_GROUP
            o_ref[:, lo:lo + FOURIER_GROUP] = res * scale


def _dft_mats(n):
    k = np.arange(n)
    ang = 2.0 * np.pi * ((k[:, None] * k[None, :]) % n) / n
    return np.cos(ang), np.sin(ang)


def _fourier(f4, *, tl=2048, kb=4):
    batch, groups, seq, ch = f4.shape
    n1 = DFT_N1
    n2 = seq // n1
    c1, s1 = _dft_mats(n1)
    m1 = jnp.asarray(np.concatenate([c1, -s1], axis=0), F32).astype(BF16)
    tw = 2.0 * np.pi * ((np.arange(n1)[:, None] * np.arange(n2)[None, :]) % seq) / seq
    tc = jnp.asarray(np.repeat(np.cos(tw), ch, axis=1), F32)
    ts = jnp.asarray(np.repeat(np.sin(tw), ch, axis=1), F32)
    c2, s2 = _dft_mats(n2)
    m2 = jnp.asarray(np.block([[c2, s2], [-s2, c2]]), F32).astype(BF16)
    cch, sch = _dft_mats(ch)
    cc = jnp.asarray(cch, F32).astype(BF16)
    sc = jnp.asarray(sch, F32).astype(BF16)

    cols = n2 * ch
    tl = min(tl, cols)
    xv = f4.reshape(batch, groups, n1, cols)
    y1 = pl.pallas_call(
        _dft_stage1_kernel,
        grid=(batch, groups, cols // tl),
        in_specs=[
            pl.BlockSpec((None, None, n1, tl), lambda b, g, c: (b, g, 0, c)),
            pl.BlockSpec((2 * n1, n1), lambda b, g, c: (0, 0)),
            pl.BlockSpec((n1, tl), lambda b, g, c: (0, c)),
            pl.BlockSpec((n1, tl), lambda b, g, c: (0, c)),
        ],
        out_specs=pl.BlockSpec((None, None, 2, n1, tl), lambda b, g, c: (b, g, 0, 0, c)),
        out_shape=jax.ShapeDtypeStruct((batch, groups, 2, n1, cols), BF16),
        compiler_params=_cparams(("parallel", "parallel", "parallel")),
        name="dft_stage1",
    )(xv, m1, tc, ts)

    yv = y1.reshape(batch, groups, 2, n1, n2, ch)
    width = groups * ch
    scale = 1.0 / math.sqrt(seq * ch)
    kern = functools.partial(_dft_stage2_kernel, kb=kb, n2=n2, scale=scale)
    out = pl.pallas_call(
        kern,
        grid=(batch, n1 // kb),
        in_specs=[
            pl.BlockSpec((None, groups, 2, kb, n2, ch), lambda b, k: (b, 0, 0, k, 0, 0)),
            pl.BlockSpec((2 * n2, 2 * n2), lambda b, k: (0, 0)),
            pl.BlockSpec((ch, ch), lambda b, k: (0, 0)),
            pl.BlockSpec((ch, ch), lambda b, k: (0, 0)),
        ],
        out_specs=pl.BlockSpec((None, n2, kb * width), lambda b, k: (b, 0, k)),
        out_shape=jax.ShapeDtypeStruct((batch, n2, n1 * width), F32),
        compiler_params=_cparams(("parallel", "parallel")),
        name="dft_stage2",
    )(yv, m2, cc, sc)
    return out.reshape(batch, seq, width)


def _out_proj_kernel(o1_ref, o2_ref, o3_ref, l1_ref, l2_ref, l3_ref, four_ref, x_ref,
                     ga_ref, gf_ref, wa_ref, wf_ref, g2_ref, e_ref, x1_ref, h2_ref):
    l1 = l1_ref[...]
    l2 = l2_ref[...]
    l3 = l3_ref[...]
    m = jnp.maximum(jnp.maximum(l1, l2), l3)
    e1 = jnp.exp(l1 - m)
    e2 = jnp.exp(l2 - m)
    e3 = jnp.exp(l3 - m)
    inv = 1.0 / (e1 + e2 + e3)
    expand = e_ref[...]

    def spread(w):
        hi = w.astype(BF16)
        lo = (w - hi.astype(F32)).astype(BF16)
        return jnp.dot(hi, expand, preferred_element_type=F32) \
            + jnp.dot(lo, expand, preferred_element_type=F32)

    attn = spread(e1 * inv) * o1_ref[...].astype(F32)
    attn = attn + spread(e2 * inv) * o2_ref[...].astype(F32)
    attn = attn + spread(e3 * inv) * o3_ref[...].astype(F32)
    a_n = _rms(attn, ga_ref[...]).astype(BF16)
    f_n = _rms(four_ref[...], gf_ref[...]).astype(BF16)
    y = jnp.dot(a_n, wa_ref[...], preferred_element_type=F32) \
        + jnp.dot(f_n, wf_ref[...], preferred_element_type=F32)
    x1 = x_ref[...] + y
    x1_ref[...] = x1
    h2_ref[...] = _rms(x1, g2_ref[...]).astype(BF16)


def _out_proj(o1, o2, o3, l1, l2, l3, four, x2, ga, gf, wa, wf, g2, *, tm=256):
    ntok, d_model = x2.shape
    aw = o1.shape[1]
    fw = four.shape[1]
    n_heads = aw // HEAD_DIM
    expand = np.zeros((LANES, aw), np.float32)
    for h in range(n_heads):
        expand[h, h * HEAD_DIM:(h + 1) * HEAD_DIM] = 1.0
    expand = jnp.asarray(expand).astype(BF16)
    row = lambda w: pl.BlockSpec((tm, w), lambda i: (i, 0))
    const = lambda a, b: pl.BlockSpec((a, b), lambda i: (0, 0))
    return pl.pallas_call(
        _out_proj_kernel,
        grid=(ntok // tm,),
        in_specs=[row(aw), row(aw), row(aw), row(LANES), row(LANES), row(LANES),
                  row(fw), row(d_model),
                  const(1, aw), const(1, fw), const(aw, d_model), const(fw, d_model),
                  const(1, d_model), const(LANES, aw)],
        out_specs=[row(d_model), row(d_model)],
        out_shape=[jax.ShapeDtypeStruct((ntok, d_model), F32),
                   jax.ShapeDtypeStruct((ntok, d_model), BF16)],
        compiler_params=_cparams(("parallel",)),
        name="out_proj",
    )(o1, o2, o3, l1, l2, l3, four, x2, ga, gf, wa, wf, g2, expand)


HALO = 16


def _up_kernel(hp_ref, hc_ref, hn_ref, wg_ref, wv_ref, cwg_ref, cwv_ref, cbg_ref, cbv_ref,
               o_ref, lhs_ref, ug_ref, uv_ref, *, tm, tiles_per_seq):
    i = pl.program_id(0)
    j = pl.program_id(1)

    @pl.when(j == 0)
    def _():
        t = i % tiles_per_seq
        zero = jnp.zeros((HALO, lhs_ref.shape[1]), BF16)
        lhs_ref[0:HALO] = jnp.where(t == 0, zero, hp_ref[...])
        lhs_ref[HALO:HALO + tm] = hc_ref[...]
        lhs_ref[HALO + tm:HALO + tm + HALO] = jnp.where(t == tiles_per_seq - 1, zero, hn_ref[...])

    lhs = lhs_ref[...]
    ug_ref[...] = jnp.dot(lhs, wg_ref[...], preferred_element_type=F32)
    uv_ref[...] = jnp.dot(lhs, wv_ref[...], preferred_element_type=F32)

    def conv(u_ref, cw_ref, cb_ref):
        cw = cw_ref[...]
        return (u_ref[HALO - 1:HALO - 1 + tm] * cw[0:1]
                + u_ref[HALO:HALO + tm] * cw[1:2]
                + u_ref[HALO + 1:HALO + 1 + tm] * cw[2:3]
                + cb_ref[...])

    gate = conv(ug_ref, cwg_ref, cbg_ref)
    val = conv(uv_ref, cwv_ref, cbv_ref)
    o_ref[...] = (gate * (1.0 / (1.0 + jnp.exp(-gate))) * val).astype(BF16)


def _up_proj(h2, w_up, conv_w, conv_b, *, seq, tm=512, tn=512):
    ntok, d_model = h2.shape
    d_ff = w_up.shape[1] // 2
    nj = d_ff // tn
    tps = seq // tm
    hb = tm // HALO
    last = ntok // HALO - 1
    kern = functools.partial(_up_kernel, tm=tm, tiles_per_seq=tps)
    return pl.pallas_call(
        kern,
        grid=(ntok // tm, nj),
        in_specs=[
            pl.BlockSpec((HALO, d_model), lambda i, j: (jnp.maximum(i * hb - 1, 0), 0)),
            pl.BlockSpec((tm, d_model), lambda i, j: (i, 0)),
            pl.BlockSpec((HALO, d_model), lambda i, j: (jnp.minimum((i + 1) * hb, last), 0)),
            pl.BlockSpec((d_model, tn), lambda i, j: (0, j)),
            pl.BlockSpec((d_model, tn), lambda i, j: (0, nj + j)),
            pl.BlockSpec((3, tn), lambda i, j: (0, j)),
            pl.BlockSpec((3, tn), lambda i, j: (0, nj + j)),
            pl.BlockSpec((1, tn), lambda i, j: (0, j)),
            pl.BlockSpec((1, tn), lambda i, j: (0, nj + j)),
        ],
        out_specs=pl.BlockSpec((tm, tn), lambda i, j: (i, j)),
        out_shape=jax.ShapeDtypeStruct((ntok, d_ff), BF16),
        scratch_shapes=[pltpu.VMEM((tm + 2 * HALO, d_model), BF16),
                        pltpu.VMEM((tm + 2 * HALO, tn), F32),
                        pltpu.VMEM((tm + 2 * HALO, tn), F32)],
        compiler_params=_cparams(("parallel", "arbitrary")),
        name="up_proj",
    )(h2, h2, h2, w_up, w_up, conv_w, conv_w, conv_b, conv_b)


def _down_kernel(a_ref, w_ref, x1_ref, g_ref, y_ref, *, tn, n_blocks):
    n = pl.program_id(1)
    y = jnp.dot(a_ref[...], w_ref[...], preferred_element_type=F32)
    for b in range(n_blocks):
        @pl.when(n == b)
        def _(b=b):
            cols = slice(b * tn, (b + 1) * tn)
            y_ref[:, cols] = x1_ref[:, cols] + y

    @pl.when(n == n_blocks - 1)
    def _():
        y_ref[...] = _rms(y_ref[...], g_ref[...])


def _down_proj(act, w_down, x1, final_g, *, tm=512, tn=512):
    ntok, d_ff = act.shape
    d_model = w_down.shape[1]
    n_blocks = d_model // tn
    kern = functools.partial(_down_kernel, tn=tn, n_blocks=n_blocks)
    return pl.pallas_call(
        kern,
        grid=(ntok // tm, n_blocks),
        in_specs=[
            pl.BlockSpec((tm, d_ff), lambda i, n: (i, 0)),
            pl.BlockSpec((d_ff, tn), lambda i, n: (0, n)),
            pl.BlockSpec((tm, d_model), lambda i, n: (i, 0)),
            pl.BlockSpec((1, d_model), lambda i, n: (0, 0)),
        ],
        out_specs=pl.BlockSpec((tm, d_model), lambda i, n: (i, 0)),
        out_shape=jax.ShapeDtypeStruct((ntok, d_model), F32),
        compiler_params=_cparams(("parallel", "arbitrary")),
        name="down_proj",
    )(act, w_down, x1, final_g)


def _rope_lane_tables(seq):
    inv_freq = ROPE_THETA ** (-jnp.arange(0, ROPE_DIM, 2, dtype=F32) / ROPE_DIM)
    ang = jnp.arange(seq, dtype=F32)[:, None] * inv_freq[None, :]
    cos, sin = jnp.cos(ang), jnp.sin(ang)
    half = ROPE_DIM // 2
    rest = HEAD_DIM - ROPE_DIM
    cos_h = jnp.concatenate([cos, cos, jnp.ones((seq, rest), F32)], axis=1)
    sa_h = jnp.concatenate([-sin, jnp.zeros((seq, HEAD_DIM - half), F32)], axis=1)
    sb_h = jnp.concatenate([jnp.zeros((seq, half), F32), sin, jnp.zeros((seq, rest), F32)], axis=1)
    reps = LANES // HEAD_DIM
    return (jnp.tile(cos_h, (1, reps)), jnp.tile(sa_h, (1, reps)), jnp.tile(sb_h, (1, reps)))


def _encoder(x, norm1_g, w_in, attn_out_g, fourier_out_g, w_out, norm2_g, w_up, conv_w, conv_b,
             w_down, final_g):
    batch, seq, d_model = x.shape
    attn_width = (w_in.shape[1] - N_FOURIER_GROUPS * FOURIER_GROUP) // 3
    ntok = batch * seq
    x2 = x.reshape(ntok, d_model)

    col_scale = jnp.where(jnp.arange(w_in.shape[1]) < attn_width, 1.0 / math.sqrt(HEAD_DIM), 1.0)
    w_in_b = (w_in * col_scale[None, :].astype(F32)).astype(BF16)
    cos_t, sa_t, sb_t = _rope_lane_tables(seq)

    proj, f4 = _in_proj(x2, norm1_g.reshape(1, -1), w_in_b, cos_t, sa_t, sb_t,
                        batch=batch, seq=seq, attn_width=attn_width)
    proj3 = proj.reshape(batch, seq, 3 * attn_width)

    outs = [_attention(proj3, dilation=d, attn_width=attn_width) for _, d in DILATION_PATTERNS]
    four = _fourier(f4)

    w_out_b = w_out.astype(BF16)
    x1, h2 = _out_proj(
        outs[0][0].reshape(ntok, attn_width), outs[1][0].reshape(ntok, attn_width),
        outs[2][0].reshape(ntok, attn_width),
        outs[0][1].reshape(ntok, LANES), outs[1][1].reshape(ntok, LANES),
        outs[2][1].reshape(ntok, LANES),
        four.reshape(ntok, -1), x2,
        attn_out_g.reshape(1, -1), fourier_out_g.reshape(1, -1),
        w_out_b[:attn_width], w_out_b[attn_width:], norm2_g.reshape(1, -1))

    act = _up_proj(h2, w_up.astype(BF16), conv_w, conv_b.reshape(1, -1), seq=seq)
    y = _down_proj(act, w_down.astype(BF16), x1, final_g.reshape(1, -1))
    return y.reshape(batch, seq, d_model)


def kernel(x_prompt, x_sample, norm1_g, w_in, attn_out_g, fourier_out_g, w_out, norm2_g, w_up,
           conv_w, conv_b, w_down, final_g):
    assert norm1_g.shape[0] == 1, "single-layer encoder"
    n_prompt = x_prompt.shape[0]
    x = jnp.concatenate([x_prompt, x_sample], axis=0)
    y = _encoder(x, norm1_g[0], w_in[0], attn_out_g[0], fourier_out_g[0], w_out[0], norm2_g[0],
                 w_up[0], conv_w[0], conv_b[0], w_down[0], final_g)
    return (y[:n_prompt], y[n_prompt:])
```

```python
import functools
import math

import numpy as np
import jax
import jax.numpy as jnp
from jax import lax
from jax.experimental import pallas as pl
from jax.experimental.pallas import tpu as pltpu

F32 = jnp.float32
BF16 = jnp.bfloat16

HEAD_DIM = 64
ROPE_DIM = 16
ROPE_THETA = 500000.0
N_FOURIER_GROUPS = 4
FOURIER_GROUP = 128
DILATIONS = (1, 4, 16)
BAND = 64
EPS = 1e-6
MASK_VALUE = -1e30
LOG2E = 1.4426950408889634
LN2 = 0.6931471805599453

LANES = 128
VMEM_LIMIT_BYTES = 56 * 1024 * 1024

DFT_N1 = 64
DFT_PAD = 8


def _cparams(semantics):
    return pltpu.CompilerParams(dimension_semantics=semantics,
                                vmem_limit_bytes=VMEM_LIMIT_BYTES)


def _rms(x, g):
    var = jnp.mean(x * x, axis=-1, keepdims=True)
    return x * lax.rsqrt(var + EPS) * g


def _in_proj_kernel(x_ref, g_ref, w_ref, cos_ref, sa_ref, sb_ref,
                    nat_ref, c4_ref, c16_ref, f_ref, h_ref, s1_ref, s2_ref,
                    *, n_rope_tiles, n_attn_tiles):
    j = pl.program_id(1)

    @pl.when(j == 0)
    def _():
        h_ref[...] = _rms(x_ref[...], g_ref[...]).astype(BF16)

    acc = jnp.dot(h_ref[...], w_ref[...], preferred_element_type=F32)
    tm = acc.shape[0]
    n_chunks = acc.shape[1] // LANES

    def emit(c, r):
        lanes = slice(c * LANES, (c + 1) * LANES)
        nat_ref[:, lanes] = r.astype(BF16)
        s1_ref[c] = r
        for r4 in range(4):
            x4 = s1_ref[c, pl.ds(r4, tm // 4, stride=4), :]
            c4_ref[r4, :, lanes] = x4.astype(BF16)
            s2_ref[c, r4] = x4
            for q4 in range(4):
                x16 = s2_ref[c, r4, pl.ds(q4, tm // 16, stride=4), :]
                c16_ref[r4 + 4 * q4, :, lanes] = x16.astype(BF16)

    @pl.when(j < n_rope_tiles)
    def _():
        cos = cos_ref[...]
        sa = sa_ref[...]
        sb = sb_ref[...]
        for c in range(n_chunks):
            a = acc[:, c * LANES:(c + 1) * LANES]
            r = a * cos + pltpu.roll(a, LANES - ROPE_DIM // 2, 1) * sa \
                + pltpu.roll(a, ROPE_DIM // 2, 1) * sb
            emit(c, r)

    @pl.when((j >= n_rope_tiles) & (j < n_attn_tiles))
    def _():
        for c in range(n_chunks):
            emit(c, acc[:, c * LANES:(c + 1) * LANES])

    @pl.when(j == n_attn_tiles)
    def _():
        f_ref[...] = acc


def _in_proj(x2, g, w, cos_t, sa_t, sb_t, *, batch, seq, attn_width, tm=512, tn=512):
    ntok, d_model = x2.shape
    width = w.shape[1]
    n_tiles = width // tn
    n_attn_tiles = 3 * attn_width // tn
    n_rope_tiles = 2 * attn_width // tn
    assert n_tiles == n_attn_tiles + 1 and tn == N_FOURIER_GROUPS * FOURIER_GROUP
    assert tm % 256 == 0
    tps = seq // tm
    aw3 = 3 * attn_width
    kern = functools.partial(_in_proj_kernel, n_rope_tiles=n_rope_tiles,
                             n_attn_tiles=n_attn_tiles)
    tab_spec = pl.BlockSpec((tm, LANES), lambda i, j: (i % tps, 0))
    jj = lambda j: jnp.minimum(j, n_attn_tiles - 1)
    return pl.pallas_call(
        kern,
        grid=(ntok // tm, n_tiles),
        in_specs=[
            pl.BlockSpec((tm, d_model), lambda i, j: (i, 0)),
            pl.BlockSpec((1, d_model), lambda i, j: (0, 0)),
            pl.BlockSpec((d_model, tn), lambda i, j: (0, j)),
            tab_spec, tab_spec, tab_spec,
        ],
        out_specs=[
            pl.BlockSpec((tm, tn), lambda i, j: (i, jj(j))),
            pl.BlockSpec((None, 4, tm // 4, tn), lambda i, j: (i // tps, 0, i % tps, jj(j))),
            pl.BlockSpec((None, 16, tm // 16, tn), lambda i, j: (i // tps, 0, i % tps, jj(j))),
            pl.BlockSpec((tm, tn), lambda i, j: (i, 0)),
        ],
        out_shape=[
            jax.ShapeDtypeStruct((ntok, aw3), BF16),
            jax.ShapeDtypeStruct((batch, 4, seq // 4, aw3), BF16),
            jax.ShapeDtypeStruct((batch, 16, seq // 16, aw3), BF16),
            jax.ShapeDtypeStruct((ntok, tn), F32),
        ],
        scratch_shapes=[pltpu.VMEM((tm, d_model), BF16),
                        pltpu.VMEM((tn // LANES, tm, LANES), F32),
                        pltpu.VMEM((tn // LANES, 4, tm // 4, LANES), F32)],
        compiler_params=_cparams(("parallel", "arbitrary")),
        name="in_proj",
    )(x2, g, w, cos_t, sa_t, sb_t)


def _attn_kernel(bias_ref, q_ref, kp_ref, kc_ref, kn_ref, vp_ref, vc_ref, vn_ref,
                 o_ref, lse_ref, kw_ref, v0_ref, v1_ref, s_ref, p_ref, m_ref,
                 *, tb, tq, n_chunks, pairs, row_chunk=32):
    j = pl.program_id(2)
    g = pl.program_id(3)
    tk = tq + 2 * BAND
    n_sub = tb // tq
    rows = tb + 2 * BAND

    kw_ref[0:BAND] = kp_ref[...]
    kw_ref[BAND:BAND + tb] = kc_ref[...]
    kw_ref[BAND + tb:rows] = kn_ref[...]

    lane = lax.broadcasted_iota(jnp.int32, (tq, LANES), 1)
    first_head = lane < HEAD_DIM
    first_chunk = lax.broadcasted_iota(jnp.int32, (row_chunk, LANES), 1) < HEAD_DIM
    first_w =lax.broadcasted_iota(jnp.int32, (rows, LANES), 1) < HEAD_DIM
    ones = jnp.ones((rows, LANES), BF16)
    for p in range(pairs):
        lanes = slice(p * LANES, (p + 1) * LANES)
        vwin = jnp.concatenate([vp_ref[:, lanes], vc_ref[:, lanes], vn_ref[:, lanes]], axis=0)
        v0_ref[p] = jnp.where(first_w, vwin, ones)
        v1_ref[p] = jnp.where(first_w, ones, vwin)

    @pl.when(g == 0)
    def _():
        lse_ref[...] = jnp.zeros(lse_ref.shape, F32)

    units = [(i, p) for i in range(n_sub) for p in range(pairs)]

    def stage_a(u):
        i, p = units[u]
        slot = u % 2
        lanes = slice(p * LANES, (p + 1) * LANES)
        q = q_ref[i * tq:(i + 1) * tq, lanes]
        zero = jnp.zeros_like(q)
        q2 = jnp.concatenate([jnp.where(first_head, q, zero), jnp.where(first_head, zero, q)],
                             axis=0)
        kw = kw_ref[i * tq:i * tq + tk, lanes]
        s = lax.dot_general(q2, kw, (((1,), (1,)), ((), ())), preferred_element_type=F32)
        if i == 0:
            idx = jnp.where(j == 0, 1, 0)
        elif i == n_sub - 1:
            idx = jnp.where(j == n_chunks - 1, 2, 0)
        else:
            idx = 0
        s_ref[slot] = s + bias_ref[idx]

    def stage_b(u):
        slot = u % 2
        for r in range(0, tq, row_chunk):
            s0 = s_ref[slot, r:r + row_chunk, :]
            s1 = s_ref[slot, tq + r:tq + r + row_chunk, :]
            m0 = jnp.max(s0, axis=-1, keepdims=True)
            m1 = jnp.max(s1, axis=-1, keepdims=True)
            p_ref[slot, r:r + row_chunk, :] = jnp.exp2(s0 - m0).astype(BF16)
            p_ref[slot, tq + r:tq + r + row_chunk, :] = jnp.exp2(s1 - m1).astype(BF16)
            m_ref[slot, r:r + row_chunk, :] = jnp.where(
                first_chunk, jnp.broadcast_to(m0, (row_chunk, LANES)),
                jnp.broadcast_to(m1, (row_chunk, LANES)))

    def stage_c(u):
        i, p = units[u]
        slot = u % 2
        lanes = slice(p * LANES, (p + 1) * LANES)
        pv0 = jnp.dot(p_ref[slot, 0:tq, :], v0_ref[p, i * tq:i * tq + tk, :],
                      preferred_element_type=F32)
        pv1 = jnp.dot(p_ref[slot, tq:2 * tq, :], v1_ref[p, i * tq:i * tq + tk, :],
                      preferred_element_type=F32)
        o = jnp.where(first_head, pv0, pv1)
        dens = pltpu.roll(jnp.where(first_head, pv1, pv0), HEAD_DIM, 1)
        o_ref[i * tq:(i + 1) * tq, lanes] = (o * (1.0 / dens)).astype(BF16)
        lse = m_ref[slot] * LN2 + jnp.log(dens)
        pair_lane = pairs * g + p
        cur = lse_ref[i * tq:(i + 1) * tq, :]
        lse_ref[i * tq:(i + 1) * tq, :] = jnp.where(
            jnp.bitwise_and(lane, HEAD_DIM - 1) == pair_lane, lse, cur)

    n_units = len(units)
    for t in range(n_units + 2):
        if t < n_units:
            stage_a(t)
        if 0 <= t - 1 < n_units:
            stage_b(t - 1)
        if 0 <= t - 2 < n_units:
            stage_c(t - 2)


def _attn_bias(tq):
    tk = tq + 2 * BAND
    col = np.arange(tk)[None, :]
    row = np.arange(tq)[:, None]
    band = (col - row >= 0) & (col - row <= 2 * BAND)
    variants = [band, band & (col >= BAND), band & (col < tq + BAND)]
    bias = np.stack([np.where(v, 0.0, MASK_VALUE) for v in variants]).astype(np.float32)
    return jnp.asarray(np.concatenate([bias, bias], axis=1))


def _attention(qkv, *, attn_width, pairs=2, tq=128, max_tb=1024):
    batch, d, t_len, width = qkv.shape
    tb = min(max_tb, t_len)
    assert t_len % tb == 0 and tb % tq == 0 and tb // tq >= 2
    n_chunks = t_len // tb
    bw = pairs * LANES
    groups = attn_width // bw
    hb = tb // BAND
    last_hb = t_len // BAND - 1
    tk = tq + 2 * BAND

    def cur(off):
        return pl.BlockSpec((None, None, tb, bw), lambda b, r, j, g: (b, r, j, off * groups + g))

    def prv(off):
        return pl.BlockSpec((None, None, BAND, bw),
                            lambda b, r, j, g: (b, r, jnp.maximum(j * hb - 1, 0), off * groups + g))

    def nxt(off):
        return pl.BlockSpec((None, None, BAND, bw),
                            lambda b, r, j, g: (b, r, jnp.minimum((j + 1) * hb, last_hb),
                                                off * groups + g))

    kern = functools.partial(_attn_kernel, tb=tb, tq=tq, n_chunks=n_chunks, pairs=pairs)
    return pl.pallas_call(
        kern,
        grid=(batch, d, n_chunks, groups),
        in_specs=[pl.BlockSpec((3, 2 * tq, tk), lambda b, r, j, g: (0, 0, 0)),
                  cur(0), prv(1), cur(1), nxt(1), prv(2), cur(2), nxt(2)],
        out_specs=[
            pl.BlockSpec((None, None, tb, bw), lambda b, r, j, g: (b, r, j, g)),
            pl.BlockSpec((None, None, tb, LANES), lambda b, r, j, g: (b, r, j, 0)),
        ],
        out_shape=[
            jax.ShapeDtypeStruct((batch, d, t_len, attn_width), BF16),
            jax.ShapeDtypeStruct((batch, d, t_len, LANES), F32),
        ],
        scratch_shapes=[pltpu.VMEM((tb + 2 * BAND, bw), BF16),
                        pltpu.VMEM((pairs, tb + 2 * BAND, LANES), BF16),
                        pltpu.VMEM((pairs, tb + 2 * BAND, LANES), BF16),
                        pltpu.VMEM((2, 2 * tq, tk), F32),
                        pltpu.VMEM((2, 2 * tq, tk), BF16),
                        pltpu.VMEM((2, tq, LANES), F32)],
        compiler_params=_cparams(("parallel", "parallel", "parallel", "arbitrary")),
        name=f"attn_d{d}",
    )(_attn_bias(tq), qkv, qkv, qkv, qkv, qkv, qkv, qkv)


def _fourier_kernel(x_ref, m1_ref, m2_ref, cc_ref, sc_ref, o_ref, yr_ref, yi_ref,
                    *, n1, n2, pitch, scale):
    def stage1(m, carry):
        xm = x_ref[pl.ds(m, n1, stride=n2), :].astype(BF16)
        y = jnp.dot(m1_ref[m], xm, preferred_element_type=F32)
        yr_ref[pl.ds(m, n1, stride=pitch), :] = y[:n1]
        yi_ref[pl.ds(m, n1, stride=pitch), :] = y[n1:]
        return carry

    lax.fori_loop(0, n2, stage1, 0, unroll=4)

    m2 = m2_ref[...]
    cc = cc_ref[...]
    sc = sc_ref[...]

    def stage2(k1, carry):
        base = pl.multiple_of(k1 * pitch, 8)
        dat = jnp.concatenate([yr_ref[pl.ds(base, n2), :], yi_ref[pl.ds(base, n2), :]],
                              axis=0).astype(BF16)
        o = jnp.dot(m2, dat, preferred_element_type=F32)
        res = jnp.dot(o[:n2].astype(BF16), cc, preferred_element_type=F32) \
            + jnp.dot(o[n2:].astype(BF16), sc, preferred_element_type=F32)
        o_ref[pl.ds(k1, n2, stride=n1), :] = res * scale
        return carry

    lax.fori_loop(0, n1, stage2, 0, unroll=2)


def _dft_mats(n):
    k = np.arange(n)
    ang = 2.0 * np.pi * ((k[:, None] * k[None, :]) % n) / n
    return np.cos(ang), np.sin(ang)


def _fourier(f3):
    batch, seq, width = f3.shape
    ch = FOURIER_GROUP
    groups = width // ch
    n1 = DFT_N1
    n2 = seq // n1
    pitch = n2 + DFT_PAD
    pos = n2 * np.arange(n1)[None, None, :] + np.arange(n2)[:, None, None]
    ang = 2.0 * np.pi * ((np.arange(n1)[None, :, None] * pos) % seq) / seq
    m1 = jnp.asarray(np.concatenate([np.cos(ang), -np.sin(ang)], axis=1), F32).astype(BF16)
    c2, s2 = _dft_mats(n2)
    m2 = jnp.asarray(np.block([[c2, s2], [-s2, c2]]), F32).astype(BF16)
    cch, sch = _dft_mats(ch)
    cc = jnp.asarray(cch, F32).astype(BF16)
    sc = jnp.asarray(sch, F32).astype(BF16)
    kern = functools.partial(_fourier_kernel, n1=n1, n2=n2, pitch=pitch,
                             scale=1.0 / math.sqrt(seq * ch))
    return pl.pallas_call(
        kern,
        grid=(batch, groups),
        in_specs=[
            pl.BlockSpec((None, seq, ch), lambda b, g: (b, 0, g)),
            pl.BlockSpec((n2, 2 * n1, n1), lambda b, g: (0, 0, 0)),
            pl.BlockSpec((2 * n2, 2 * n2), lambda b, g: (0, 0)),
            pl.BlockSpec((ch, ch), lambda b, g: (0, 0)),
            pl.BlockSpec((ch, ch), lambda b, g: (0, 0)),
        ],
        out_specs=pl.BlockSpec((None, seq, ch), lambda b, g: (b, 0, g)),
        out_shape=jax.ShapeDtypeStruct((batch, seq, width), F32),
        scratch_shapes=[pltpu.VMEM((n1 * pitch, ch), F32),
                        pltpu.VMEM((n1 * pitch, ch), F32)],
        compiler_params=_cparams(("parallel", "parallel")),
        name="fourier",
    )(f3, m1, m2, cc, sc)


def _out_proj_kernel(o1_ref, o4_ref, o16_ref, l1_ref, l4_ref, l16_ref, four_ref, x_ref,
                     ga_ref, gf_ref, wa_ref, wf_ref, g2_ref, e_ref, x1_ref, h2_ref,
                     n4_ref, n16_ref, ln4_ref, ln16_ref):
    tm = x_ref.shape[0]
    n_slabs = n4_ref.shape[0]

    for r in range(4):
        blk = o4_ref[r].astype(F32)
        for c in range(n_slabs):
            n4_ref[c, pl.ds(r, tm // 4, stride=4), :] = blk[:, c * LANES:(c + 1) * LANES]
        ln4_ref[pl.ds(r, tm // 4, stride=4), :] = l4_ref[r]
    for r in range(16):
        blk = o16_ref[r].astype(F32)
        for c in range(n_slabs):
            n16_ref[c, pl.ds(r, tm // 16, stride=16), :] = blk[:, c * LANES:(c + 1) * LANES]
        ln16_ref[pl.ds(r, tm // 16, stride=16), :] = l16_ref[r]

    l1 = l1_ref[...]
    l2 = ln4_ref[...]
    l3 = ln16_ref[...]
    m = jnp.maximum(jnp.maximum(l1, l2), l3)
    e1 = jnp.exp(l1 - m)
    e2 = jnp.exp(l2 - m)
    e3 = jnp.exp(l3 - m)
    inv = 1.0 / (e1 + e2 + e3)
    expand = e_ref[...]

    def spread(w):
        hi = w.astype(BF16)
        lo = (w - hi.astype(F32)).astype(BF16)
        return jnp.dot(hi, expand, preferred_element_type=F32) \
            + jnp.dot(lo, expand, preferred_element_type=F32)

    w1 = spread(e1 * inv)
    w2 = spread(e2 * inv)
    w3 = spread(e3 * inv)
    o1 = o1_ref[...].astype(F32)
    slabs = []
    ssq = jnp.zeros((tm, 1), F32)
    for c in range(n_slabs):
        lanes = slice(c * LANES, (c + 1) * LANES)
        a = w1[:, lanes] * o1[:, lanes] + w2[:, lanes] * n4_ref[c] + w3[:, lanes] * n16_ref[c]
        ssq = ssq + jnp.sum(a * a, axis=-1, keepdims=True)
        slabs.append(a)
    attn = jnp.concatenate(slabs, axis=1)
    width = n_slabs * LANES
    a_n = (attn * lax.rsqrt(ssq / width + EPS) * ga_ref[...]).astype(BF16)
    f_n = _rms(four_ref[...], gf_ref[...]).astype(BF16)
    y = jnp.dot(a_n, wa_ref[...], preferred_element_type=F32) \
        + jnp.dot(f_n, wf_ref[...], preferred_element_type=F32)
    x1 = x_ref[...] + y
    x1_ref[...] = x1
    h2_ref[...] = _rms(x1, g2_ref[...]).astype(BF16)


def _out_proj(o1, o4, o16, l1, l4, l16, four, x2, ga, gf, wa, wf, g2, *, seq, tm=256):
    ntok, d_model = x2.shape
    aw = o1.shape[1]
    fw = four.shape[1]
    n_heads = aw // HEAD_DIM
    tps = seq // tm
    expand = np.zeros((LANES, aw), np.float32)
    for h in range(n_heads):
        expand[(h % 2) * HEAD_DIM + h // 2, h * HEAD_DIM:(h + 1) * HEAD_DIM] = 1.0
    expand = jnp.asarray(expand).astype(BF16)
    row = lambda w: pl.BlockSpec((tm, w), lambda i: (i, 0))
    cls = lambda d, w: pl.BlockSpec((None, d, tm // d, w), lambda i: (i // tps, 0, i % tps, 0))
    const = lambda a, b: pl.BlockSpec((a, b), lambda i: (0, 0))
    return pl.pallas_call(
        _out_proj_kernel,
        grid=(ntok // tm,),
        in_specs=[row(aw), cls(4, aw), cls(16, aw), row(LANES), cls(4, LANES), cls(16, LANES),
                  row(fw), row(d_model),
                  const(1, aw), const(1, fw), const(aw, d_model), const(fw, d_model),
                  const(1, d_model), const(LANES, aw)],
        out_specs=[row(d_model), row(d_model)],
        out_shape=[jax.ShapeDtypeStruct((ntok, d_model), F32),
                   jax.ShapeDtypeStruct((ntok, d_model), BF16)],
        scratch_shapes=[pltpu.VMEM((aw // LANES, tm, LANES), F32),
                        pltpu.VMEM((aw // LANES, tm, LANES), F32),
                        pltpu.VMEM((tm, LANES), F32),
                        pltpu.VMEM((tm, LANES), F32)],
        compiler_params=_cparams(("parallel",)),
        name="out_proj",
    )(o1, o4, o16, l1, l4, l16, four, x2, ga, gf, wa, wf, g2, expand)


HALO = 16


def _up_kernel(hp_ref, hc_ref, hn_ref, wg_ref, wv_ref, cwg_ref, cwv_ref, cbg_ref, cbv_ref,
               o_ref, lhs_ref, ug_ref, uv_ref, *, tm, tiles_per_seq):
    i = pl.program_id(0)
    j = pl.program_id(1)

    @pl.when(j == 0)
    def _():
        t = i % tiles_per_seq
        zero = jnp.zeros((HALO, lhs_ref.shape[1]), BF16)
        lhs_ref[0:HALO] = jnp.where(t == 0, zero, hp_ref[...])
        lhs_ref[HALO:HALO + tm] = hc_ref[...]
        lhs_ref[HALO + tm:HALO + tm + HALO] = jnp.where(t == tiles_per_seq - 1, zero, hn_ref[...])

    lhs = lhs_ref[...]
    ug_ref[...] = jnp.dot(lhs, wg_ref[...], preferred_element_type=F32)
    uv_ref[...] = jnp.dot(lhs, wv_ref[...], preferred_element_type=F32)

    def conv(u_ref, cw_ref, cb_ref):
        cw = cw_ref[...]
        return (u_ref[HALO - 1:HALO - 1 + tm] * cw[0:1]
                + u_ref[HALO:HALO + tm] * cw[1:2]
                + u_ref[HALO + 1:HALO + 1 + tm] * cw[2:3]
                + cb_ref[...])

    gate = conv(ug_ref, cwg_ref, cbg_ref)
    val = conv(uv_ref, cwv_ref, cbv_ref)
    o_ref[...] = (gate * (1.0 / (1.0 + jnp.exp(-gate))) * val).astype(BF16)


def _up_proj(h2, w_up, conv_w, conv_b, *, seq, tm=512, tn=512):
    ntok, d_model = h2.shape
    d_ff = w_up.shape[1] // 2
    nj = d_ff // tn
    tps = seq // tm
    hb = tm // HALO
    last = ntok // HALO - 1
    kern = functools.partial(_up_kernel, tm=tm, tiles_per_seq=tps)
    return pl.pallas_call(
        kern,
        grid=(ntok // tm, nj),
        in_specs=[
            pl.BlockSpec((HALO, d_model), lambda i, j: (jnp.maximum(i * hb - 1, 0), 0)),
            pl.BlockSpec((tm, d_model), lambda i, j: (i, 0)),
            pl.BlockSpec((HALO, d_model), lambda i, j: (jnp.minimum((i + 1) * hb, last), 0)),
            pl.BlockSpec((d_model, tn), lambda i, j: (0, j)),
            pl.BlockSpec((d_model, tn), lambda i, j: (0, nj + j)),
            pl.BlockSpec((3, tn), lambda i, j: (0, j)),
            pl.BlockSpec((3, tn), lambda i, j: (0, nj + j)),
            pl.BlockSpec((1, tn), lambda i, j: (0, j)),
            pl.BlockSpec((1, tn), lambda i, j: (0, nj + j)),
        ],
        out_specs=pl.BlockSpec((tm, tn), lambda i, j: (i, j)),
        out_shape=jax.ShapeDtypeStruct((ntok, d_ff), BF16),
        scratch_shapes=[pltpu.VMEM((tm + 2 * HALO, d_model), BF16),
                        pltpu.VMEM((tm + 2 * HALO, tn), F32),
                        pltpu.VMEM((tm + 2 * HALO, tn), F32)],
        compiler_params=_cparams(("parallel", "arbitrary")),
        name="up_proj",
    )(h2, h2, h2, w_up, w_up, conv_w, conv_w, conv_b, conv_b)


def _down_kernel(a_ref, w_ref, x1_ref, g_ref, y_ref, *, tn, n_blocks):
    n = pl.program_id(1)
    y = jnp.dot(a_ref[...], w_ref[...], preferred_element_type=F32)
    for b in range(n_blocks):
        @pl.when(n == b)
        def _(b=b):
            cols = slice(b * tn, (b + 1) * tn)
            y_ref[:, cols] = x1_ref[:, cols] + y

    @pl.when(n == n_blocks - 1)
    def _():
        y_ref[...] = _rms(y_ref[...], g_ref[...])


def _down_proj(act, w_down, x1, final_g, *, tm=512, tn=512):
    ntok, d_ff = act.shape
    d_model = w_down.shape[1]
    n_blocks = d_model // tn
    kern = functools.partial(_down_kernel, tn=tn, n_blocks=n_blocks)
    return pl.pallas_call(
        kern,
        grid=(ntok // tm, n_blocks),
        in_specs=[
            pl.BlockSpec((tm, d_ff), lambda i, n: (i, 0)),
            pl.BlockSpec((d_ff, tn), lambda i, n: (0, n)),
            pl.BlockSpec((tm, d_model), lambda i, n: (i, 0)),
            pl.BlockSpec((1, d_model), lambda i, n: (0, 0)),
        ],
        out_specs=pl.BlockSpec((tm, d_model), lambda i, n: (i, 0)),
        out_shape=jax.ShapeDtypeStruct((ntok, d_model), F32),
        compiler_params=_cparams(("parallel", "arbitrary")),
        name="down_proj",
    )(act, w_down, x1, final_g)


def _rope_lane_tables(seq):
    inv_freq = ROPE_THETA ** (-jnp.arange(0, ROPE_DIM, 2, dtype=F32) / ROPE_DIM)
    ang = jnp.arange(seq, dtype=F32)[:, None] * inv_freq[None, :]
    cos, sin = jnp.cos(ang), jnp.sin(ang)
    half = ROPE_DIM // 2
    rest = HEAD_DIM - ROPE_DIM
    cos_h = jnp.concatenate([cos, cos, jnp.ones((seq, rest), F32)], axis=1)
    sa_h = jnp.concatenate([-sin, jnp.zeros((seq, HEAD_DIM - half), F32)], axis=1)
    sb_h = jnp.concatenate([jnp.zeros((seq, half), F32), sin, jnp.zeros((seq, rest), F32)], axis=1)
    reps = LANES // HEAD_DIM
    return (jnp.tile(cos_h, (1, reps)), jnp.tile(sa_h, (1, reps)), jnp.tile(sb_h, (1, reps)))


def _encoder(x, p):
    batch, seq, d_model = x.shape
    aw = p["attn_width"]
    ntok = batch * seq
    x2 = x.reshape(ntok, d_model)

    nat, c4, c16, f = _in_proj(x2, p["norm1_g"], p["w_in"], *p["rope"],
                               batch=batch, seq=seq, attn_width=aw)
    o1, l1 = _attention(nat.reshape(batch, 1, seq, 3 * aw), attn_width=aw)
    o4, l4 = _attention(c4, attn_width=aw)
    o16, l16 = _attention(c16, attn_width=aw)
    four = _fourier(f.reshape(batch, seq, -1))

    x1, h2 = _out_proj(o1.reshape(ntok, aw), o4, o16, l1.reshape(ntok, LANES), l4, l16,
                       four.reshape(ntok, -1), x2,
                       p["attn_out_g"], p["fourier_out_g"], p["w_out_a"], p["w_out_f"],
                       p["norm2_g"], seq=seq)
    act = _up_proj(h2, p["w_up"], p["conv_w"], p["conv_b"], seq=seq)
    y = _down_proj(act, p["w_down"], x1, p["final_g"])
    return y.reshape(batch, seq, d_model)


def _prepare(seq, norm1_g, w_in, attn_out_g, fourier_out_g, w_out, norm2_g, w_up, conv_w, conv_b,
             w_down, final_g):
    aw = (w_in.shape[1] - N_FOURIER_GROUPS * FOURIER_GROUP) // 3
    col_scale = jnp.where(jnp.arange(w_in.shape[1]) < aw, LOG2E / math.sqrt(HEAD_DIM), 1.0)
    w_out_b = w_out.astype(BF16)
    return {
        "attn_width": aw,
        "rope": _rope_lane_tables(seq),
        "norm1_g": norm1_g.reshape(1, -1),
        "w_in": (w_in * col_scale[None, :].astype(F32)).astype(BF16),
        "attn_out_g": attn_out_g.reshape(1, -1),
        "fourier_out_g": fourier_out_g.reshape(1, -1),
        "w_out_a": w_out_b[:aw],
        "w_out_f": w_out_b[aw:],
        "norm2_g": norm2_g.reshape(1, -1),
        "w_up": w_up.astype(BF16),
        "conv_w": conv_w,
        "conv_b": conv_b.reshape(1, -1),
        "w_down": w_down.astype(BF16),
        "final_g": final_g.reshape(1, -1),
    }


def kernel(x_prompt, x_sample, norm1_g, w_in, attn_out_g, fourier_out_g, w_out, norm2_g, w_up,
           conv_w, conv_b, w_down, final_g):
    assert norm1_g.shape[0] == 1, "single-layer encoder"
    assert x_prompt.shape[1] == x_sample.shape[1]
    p = _prepare(x_prompt.shape[1], norm1_g[0], w_in[0], attn_out_g[0], fourier_out_g[0], w_out[0],
                 norm2_g[0], w_up[0], conv_w[0], conv_b[0], w_down[0], final_g)
    return (_encoder(x_prompt, p), _encoder(x_sample, p))
```

```python
import functools
import math

import numpy as np
import jax
import jax.numpy as jnp
from jax import lax
from jax.experimental import pallas as pl
from jax.experimental.pallas import tpu as pltpu

F32 = jnp.float32
BF16 = jnp.bfloat16

HEAD_DIM = 64
ROPE_DIM = 16
ROPE_THETA = 500000.0
N_FOURIER_GROUPS = 4
FOURIER_GROUP = 128
DILATIONS = (1, 4, 16)
BAND = 64
EPS = 1e-6
MASK_VALUE = -1e30
LOG2E = 1.4426950408889634
LN2 = 0.6931471805599453

LANES = 128
VMEM_LIMIT_BYTES = 56 * 1024 * 1024

DFT_N1 = 64
DFT_PAD = 8


def _cparams(semantics):
    return pltpu.CompilerParams(dimension_semantics=semantics,
                                vmem_limit_bytes=VMEM_LIMIT_BYTES)


def _rms(x, g):
    var = jnp.mean(x * x, axis=-1, keepdims=True)
    return x * lax.rsqrt(var + EPS) * g


def _in_proj_kernel(x_ref, g_ref, w_ref, tab_ref, nat_ref, c4_ref, c16_ref,
                    h_ref, acc_ref, s1_ref, s2_ref, *, nj):
    n = pl.program_id(0)
    slot = n % 2
    tm = acc_ref.shape[1]
    n_chunks = acc_ref.shape[2] // LANES

    @pl.when(n == 0)
    def _():
        acc_ref[1] = jnp.zeros(acc_ref.shape[1:], F32)

    @pl.when(n % nj == 0)
    def _():
        h_ref[...] = _rms(x_ref[...], g_ref[...]).astype(BF16)

    cos = tab_ref[0]
    sa = tab_ref[1]
    sb = tab_ref[2]
    for c in range(n_chunks):
        lanes = slice(c * LANES, (c + 1) * LANES)
        a = acc_ref[1 - slot, :, lanes]
        r = a * cos + pltpu.roll(a, LANES - ROPE_DIM // 2, 1) * sa \
            + pltpu.roll(a, ROPE_DIM // 2, 1) * sb
        nat_ref[:, lanes] = r.astype(BF16)
        s1_ref[c] = r
        for r4 in range(4):
            x4 = s1_ref[c, pl.ds(r4, tm // 4, stride=4), :]
            c4_ref[r4, :, lanes] = x4.astype(BF16)
            s2_ref[c, r4] = x4
            for q4 in range(4):
                x16 = s2_ref[c, r4, pl.ds(q4, tm // 16, stride=4), :]
                c16_ref[r4 + 4 * q4, :, lanes] = x16.astype(BF16)

    acc_ref[slot] = jnp.dot(h_ref[...], w_ref[...], preferred_element_type=F32)


def _in_proj(x2, g, w, tabs, *, batch, seq, n_rope_tiles, tm=512, tn=512):
    ntok, d_model = x2.shape
    width = w.shape[1]
    nj = width // tn
    ntiles = ntok // tm
    assert tm % 256 == 0
    tps = seq // tm
    kern = functools.partial(_in_proj_kernel, nj=nj)

    def prev(n):
        pn = jnp.maximum(n - 1, 0)
        return pn // nj, pn % nj

    def tab_map(n):
        pi, pj = prev(n)
        return (jnp.where(pj < n_rope_tiles, 0, 1), 0, pi % tps, 0)

    def nat_map(n):
        return prev(n)

    def cls_map(n):
        pi, pj = prev(n)
        return (pi // tps, 0, pi % tps, pj)

    return pl.pallas_call(
        kern,
        grid=(ntiles * nj + 1,),
        in_specs=[
            pl.BlockSpec((tm, d_model), lambda n: (jnp.minimum(n // nj, ntiles - 1), 0)),
            pl.BlockSpec((1, d_model), lambda n: (0, 0)),
            pl.BlockSpec((d_model, tn), lambda n: (0, n % nj)),
            pl.BlockSpec((None, 3, tm, LANES), tab_map),
        ],
        out_specs=[
            pl.BlockSpec((tm, tn), nat_map),
            pl.BlockSpec((None, 4, tm // 4, tn), cls_map),
            pl.BlockSpec((None, 16, tm // 16, tn), cls_map),
        ],
        out_shape=[
            jax.ShapeDtypeStruct((ntok, width), BF16),
            jax.ShapeDtypeStruct((batch, 4, seq // 4, width), BF16),
            jax.ShapeDtypeStruct((batch, 16, seq // 16, width), BF16),
        ],
        scratch_shapes=[pltpu.VMEM((tm, d_model), BF16),
                        pltpu.VMEM((2, tm, tn), F32),
                        pltpu.VMEM((tn // LANES, tm, LANES), F32),
                        pltpu.VMEM((tn // LANES, 4, tm // 4, LANES), F32)],
        compiler_params=_cparams(("arbitrary",)),
        name="in_proj",
    )(x2, g, w, tabs)


def _attn_kernel(bias_ref, q_ref, kp_ref, kc_ref, kn_ref, vp_ref, vc_ref, vn_ref,
                 o_ref, lse_ref, v0_ref, v1_ref, s_ref, p_ref, m_ref,
                 *, tb, tq, n_chunks, pairs, row_chunk=32):
    j = pl.program_id(2)
    tk = tq + 2 * BAND
    n_sub = tb // tq

    lane = lax.broadcasted_iota(jnp.int32, (tq, LANES), 1)
    first_head = lane < HEAD_DIM
    first_chunk = lax.broadcasted_iota(jnp.int32, (row_chunk, LANES), 1) < HEAD_DIM
    first_w = lax.broadcasted_iota(jnp.int32, (tk, LANES), 1) < HEAD_DIM
    ones = jnp.ones((tk, LANES), BF16)

    def window(prev_ref, cur_ref, next_ref, i, lanes):
        if i == 0:
            return jnp.concatenate([prev_ref[:, lanes], cur_ref[0:tq + BAND, lanes]], axis=0)
        if i == n_sub - 1:
            return jnp.concatenate([cur_ref[tb - tq - BAND:tb, lanes], next_ref[:, lanes]], axis=0)
        return cur_ref[i * tq - BAND:(i + 1) * tq + BAND, lanes]

    units = [(i, p) for i in range(n_sub) for p in range(pairs)]

    def stage_a(u):
        i, p = units[u]
        slot = u % 2
        lanes = slice(p * LANES, (p + 1) * LANES)
        vwin = window(vp_ref, vc_ref, vn_ref, i, lanes)
        v0_ref[u % 3] = jnp.where(first_w, vwin, ones)
        v1_ref[u % 3] = jnp.where(first_w, ones, vwin)
        q = q_ref[i * tq:(i + 1) * tq, lanes]
        zero = jnp.zeros_like(q)
        q2 = jnp.concatenate([jnp.where(first_head, q, zero), jnp.where(first_head, zero, q)],
                             axis=0)
        kw = window(kp_ref, kc_ref, kn_ref, i, lanes)
        s = lax.dot_general(q2, kw, (((1,), (1,)), ((), ())), preferred_element_type=F32)
        if i == 0:
            idx = jnp.where(j == 0, 1, 0)
        elif i == n_sub - 1:
            idx = jnp.where(j == n_chunks - 1, 2, 0)
        else:
            idx = 0
        s_ref[slot] = s + bias_ref[idx]

    def stage_b(u):
        slot = u % 2
        for r in range(0, tq, row_chunk):
            s0 = s_ref[slot, r:r + row_chunk, :]
            s1 = s_ref[slot, tq + r:tq + r + row_chunk, :]
            m0 = jnp.max(s0, axis=-1, keepdims=True)
            m1 = jnp.max(s1, axis=-1, keepdims=True)
            p_ref[slot, r:r + row_chunk, :] = jnp.exp2(s0 - m0).astype(BF16)
            p_ref[slot, tq + r:tq + r + row_chunk, :] = jnp.exp2(s1 - m1).astype(BF16)
            m_ref[slot, r:r + row_chunk, :] = jnp.where(
                first_chunk, jnp.broadcast_to(m0, (row_chunk, LANES)),
                jnp.broadcast_to(m1, (row_chunk, LANES)))

    def stage_c(u):
        i, p = units[u]
        slot = u % 2
        lanes = slice(p * LANES, (p + 1) * LANES)
        pv0 = jnp.dot(p_ref[slot, 0:tq, :], v0_ref[u % 3], preferred_element_type=F32)
        pv1 = jnp.dot(p_ref[slot, tq:2 * tq, :], v1_ref[u % 3], preferred_element_type=F32)
        o = jnp.where(first_head, pv0, pv1)
        dens = pltpu.roll(jnp.where(first_head, pv1, pv0), HEAD_DIM, 1)
        o_ref[i * tq:(i + 1) * tq, lanes] = (o * (1.0 / dens)).astype(BF16)
        lse = m_ref[slot] * LN2 + jnp.log(dens)
        cur = jnp.zeros((tq, LANES), F32) if p == 0 else lse_ref[i * tq:(i + 1) * tq, :]
        lse_ref[i * tq:(i + 1) * tq, :] = jnp.where(
            jnp.bitwise_and(lane, HEAD_DIM - 1) == p, lse, cur)

    n_units = len(units)
    for t in range(n_units + 2):
        if t < n_units:
            stage_a(t)
        if 0 <= t - 1 < n_units:
            stage_b(t - 1)
        if 0 <= t - 2 < n_units:
            stage_c(t - 2)


def _attn_bias(tq):
    tk = tq + 2 * BAND
    col = np.arange(tk)[None, :]
    row = np.arange(tq)[:, None]
    band = (col - row >= 0) & (col - row <= 2 * BAND)
    variants = [band, band & (col >= BAND), band & (col < tq + BAND)]
    bias = np.stack([np.where(v, 0.0, MASK_VALUE) for v in variants]).astype(np.float32)
    return jnp.asarray(np.concatenate([bias, bias], axis=1))


def _attention(qkv, *, attn_width, tq=128, max_tb=512):
    batch, d, t_len, width = qkv.shape
    tb = min(max_tb, t_len)
    assert t_len % tb == 0 and tb % tq == 0 and tb // tq >= 2
    n_chunks = t_len // tb
    pairs = attn_width // LANES
    hb = tb // BAND
    last_hb = t_len // BAND - 1
    tk = tq + 2 * BAND

    def cur(off):
        return pl.BlockSpec((None, None, tb, attn_width), lambda b, r, j: (b, r, j, off))

    def prv(off):
        return pl.BlockSpec((None, None, BAND, attn_width),
                            lambda b, r, j: (b, r, jnp.maximum(j * hb - 1, 0), off))

    def nxt(off):
        return pl.BlockSpec((None, None, BAND, attn_width),
                            lambda b, r, j: (b, r, jnp.minimum((j + 1) * hb, last_hb), off))

    kern = functools.partial(_attn_kernel, tb=tb, tq=tq, n_chunks=n_chunks, pairs=pairs)
    return pl.pallas_call(
        kern,
        grid=(batch, d, n_chunks),
        in_specs=[pl.BlockSpec((3, 2 * tq, tk), lambda b, r, j: (0, 0, 0)),
                  cur(0), prv(1), cur(1), nxt(1), prv(2), cur(2), nxt(2)],
        out_specs=[
            pl.BlockSpec((None, None, tb, attn_width), lambda b, r, j: (b, r, j, 0)),
            pl.BlockSpec((None, None, tb, LANES), lambda b, r, j: (b, r, j, 0)),
        ],
        out_shape=[
            jax.ShapeDtypeStruct((batch, d, t_len, attn_width), BF16),
            jax.ShapeDtypeStruct((batch, d, t_len, LANES), F32),
        ],
        scratch_shapes=[pltpu.VMEM((3, tk, LANES), BF16),
                        pltpu.VMEM((3, tk, LANES), BF16),
                        pltpu.VMEM((2, 2 * tq, tk), F32),
                        pltpu.VMEM((2, 2 * tq, tk), BF16),
                        pltpu.VMEM((2, tq, LANES), F32)],
        compiler_params=_cparams(("parallel", "parallel", "parallel")),
        name=f"attn_d{d}",
    )(_attn_bias(tq), qkv, qkv, qkv, qkv, qkv, qkv, qkv)


def _fourier_kernel(x_ref, m1_ref, m2_ref, cc_ref, sc_ref, o_ref, xf_ref, yr_ref, yi_ref,
                    *, n1, n2, pitch, scale):
    xf_ref[...] = x_ref[...].astype(F32)

    def stage1(m, carry):
        xm = xf_ref[pl.ds(m, n1, stride=n2), :].astype(BF16)
        y = jnp.dot(m1_ref[m], xm, preferred_element_type=F32)
        yr_ref[pl.ds(m, n1, stride=pitch), :] = y[:n1]
        yi_ref[pl.ds(m, n1, stride=pitch), :] = y[n1:]
        return carry

    lax.fori_loop(0, n2, stage1, 0, unroll=4)

    m2 = m2_ref[...]
    cc = cc_ref[...]
    sc = sc_ref[...]

    def stage2(k1, carry):
        base = pl.multiple_of(k1 * pitch, 8)
        dat = jnp.concatenate([yr_ref[pl.ds(base, n2), :], yi_ref[pl.ds(base, n2), :]],
                              axis=0).astype(BF16)
        o = jnp.dot(m2, dat, preferred_element_type=F32)
        res = jnp.dot(o[:n2].astype(BF16), cc, preferred_element_type=F32) \
            + jnp.dot(o[n2:].astype(BF16), sc, preferred_element_type=F32)
        o_ref[pl.ds(k1, n2, stride=n1), :] = res * scale
        return carry

    lax.fori_loop(0, n1, stage2, 0, unroll=2)


def _dft_mats(n):
    k = np.arange(n)
    ang = 2.0 * np.pi * ((k[:, None] * k[None, :]) % n) / n
    return np.cos(ang), np.sin(ang)


def _fourier(f3, *, first_lane_block=0):
    batch, seq, _ = f3.shape
    ch = FOURIER_GROUP
    groups = N_FOURIER_GROUPS
    width = groups * ch
    n1 = DFT_N1
    n2 = seq // n1
    pitch = n2 + DFT_PAD
    pos = n2 * np.arange(n1)[None, None, :] + np.arange(n2)[:, None, None]
    ang = 2.0 * np.pi * ((np.arange(n1)[None, :, None] * pos) % seq) / seq
    m1 = jnp.asarray(np.concatenate([np.cos(ang), -np.sin(ang)], axis=1), F32).astype(BF16)
    c2, s2 = _dft_mats(n2)
    m2 = jnp.asarray(np.block([[c2, s2], [-s2, c2]]), F32).astype(BF16)
    cch, sch = _dft_mats(ch)
    cc = jnp.asarray(cch, F32).astype(BF16)
    sc = jnp.asarray(sch, F32).astype(BF16)
    kern = functools.partial(_fourier_kernel, n1=n1, n2=n2, pitch=pitch,
                             scale=1.0 / math.sqrt(seq * ch))
    return pl.pallas_call(
        kern,
        grid=(batch, groups),
        in_specs=[
            pl.BlockSpec((None, seq, ch), lambda b, g: (b, 0, first_lane_block + g)),
            pl.BlockSpec((n2, 2 * n1, n1), lambda b, g: (0, 0, 0)),
            pl.BlockSpec((2 * n2, 2 * n2), lambda b, g: (0, 0)),
            pl.BlockSpec((ch, ch), lambda b, g: (0, 0)),
            pl.BlockSpec((ch, ch), lambda b, g: (0, 0)),
        ],
        out_specs=pl.BlockSpec((None, seq, ch), lambda b, g: (b, 0, g)),
        out_shape=jax.ShapeDtypeStruct((batch, seq, width), F32),
        scratch_shapes=[pltpu.VMEM((seq, ch), F32),
                        pltpu.VMEM((n1 * pitch, ch), F32),
                        pltpu.VMEM((n1 * pitch, ch), F32)],
        compiler_params=_cparams(("parallel", "parallel")),
        name="fourier",
    )(f3, m1, m2, cc, sc)


def _out_proj_kernel(o1_ref, o4_ref, o16_ref, l1_ref, l4_ref, l16_ref, four_ref, x_ref,
                     ga_ref, gf_ref, wa_ref, wf_ref, g2_ref, e_ref, x1_ref, h2_ref,
                     n4_ref, n16_ref, ln4_ref, ln16_ref):
    tm = x_ref.shape[0]
    n_slabs = n4_ref.shape[0]

    for r in range(4):
        blk = o4_ref[r].astype(F32)
        for c in range(n_slabs):
            n4_ref[c, pl.ds(r, tm // 4, stride=4), :] = blk[:, c * LANES:(c + 1) * LANES]
        ln4_ref[pl.ds(r, tm // 4, stride=4), :] = l4_ref[r]
    for r in range(16):
        blk = o16_ref[r].astype(F32)
        for c in range(n_slabs):
            n16_ref[c, pl.ds(r, tm // 16, stride=16), :] = blk[:, c * LANES:(c + 1) * LANES]
        ln16_ref[pl.ds(r, tm // 16, stride=16), :] = l16_ref[r]

    l1 = l1_ref[...]
    l2 = ln4_ref[...]
    l3 = ln16_ref[...]
    m = jnp.maximum(jnp.maximum(l1, l2), l3)
    e1 = jnp.exp(l1 - m)
    e2 = jnp.exp(l2 - m)
    e3 = jnp.exp(l3 - m)
    inv = 1.0 / (e1 + e2 + e3)
    expand = e_ref[...]

    def spread(w):
        hi = w.astype(BF16)
        lo = (w - hi.astype(F32)).astype(BF16)
        return jnp.dot(hi, expand, preferred_element_type=F32) \
            + jnp.dot(lo, expand, preferred_element_type=F32)

    w1 = spread(e1 * inv)
    w2 = spread(e2 * inv)
    w3 = spread(e3 * inv)
    o1 = o1_ref[...].astype(F32)
    slabs = []
    ssq = jnp.zeros((tm, 1), F32)
    for c in range(n_slabs):
        lanes = slice(c * LANES, (c + 1) * LANES)
        a = w1[:, lanes] * o1[:, lanes] + w2[:, lanes] * n4_ref[c] + w3[:, lanes] * n16_ref[c]
        ssq = ssq + jnp.sum(a * a, axis=-1, keepdims=True)
        slabs.append(a)
    attn = jnp.concatenate(slabs, axis=1)
    width = n_slabs * LANES
    a_n = (attn * lax.rsqrt(ssq / width + EPS) * ga_ref[...]).astype(BF16)
    f_n = _rms(four_ref[...], gf_ref[...]).astype(BF16)
    y = jnp.dot(a_n, wa_ref[...], preferred_element_type=F32) \
        + jnp.dot(f_n, wf_ref[...], preferred_element_type=F32)
    x1 = x_ref[...] + y
    x1_ref[...] = x1
    h2_ref[...] = _rms(x1, g2_ref[...]).astype(BF16)


def _out_proj(o1, o4, o16, l1, l4, l16, four, x2, ga, gf, wa, wf, g2, *, seq, tm=256):
    ntok, d_model = x2.shape
    aw = o1.shape[1]
    fw = four.shape[1]
    n_heads = aw // HEAD_DIM
    tps = seq // tm
    expand = np.zeros((LANES, aw), np.float32)
    for h in range(n_heads):
        expand[(h % 2) * HEAD_DIM + h // 2, h * HEAD_DIM:(h + 1) * HEAD_DIM] = 1.0
    expand = jnp.asarray(expand).astype(BF16)
    row = lambda w: pl.BlockSpec((tm, w), lambda i: (i, 0))
    cls = lambda d, w: pl.BlockSpec((None, d, tm // d, w), lambda i: (i // tps, 0, i % tps, 0))
    const = lambda a, b: pl.BlockSpec((a, b), lambda i: (0, 0))
    return pl.pallas_call(
        _out_proj_kernel,
        grid=(ntok // tm,),
        in_specs=[row(aw), cls(4, aw), cls(16, aw), row(LANES), cls(4, LANES), cls(16, LANES),
                  row(fw), row(d_model),
                  const(1, aw), const(1, fw), const(aw, d_model), const(fw, d_model),
                  const(1, d_model), const(LANES, aw)],
        out_specs=[row(d_model), row(d_model)],
        out_shape=[jax.ShapeDtypeStruct((ntok, d_model), F32),
                   jax.ShapeDtypeStruct((ntok, d_model), BF16)],
        scratch_shapes=[pltpu.VMEM((aw // LANES, tm, LANES), F32),
                        pltpu.VMEM((aw // LANES, tm, LANES), F32),
                        pltpu.VMEM((tm, LANES), F32),
                        pltpu.VMEM((tm, LANES), F32)],
        compiler_params=_cparams(("parallel",)),
        name="out_proj",
    )(o1, o4, o16, l1, l4, l16, four, x2, ga, gf, wa, wf, g2, expand)


HALO = 16


def _up_kernel(hp_ref, hc_ref, hn_ref, wg_ref, wv_ref, cwg_ref, cwv_ref, cbg_ref, cbv_ref,
               o_ref, lhs_ref, ug_ref, uv_ref, *, tm, nj, n_tiles, tiles_per_seq):
    n = pl.program_id(0)
    slot = n % 2

    @pl.when(n == 0)
    def _():
        zero = jnp.zeros(ug_ref.shape[1:], F32)
        ug_ref[1] = zero
        uv_ref[1] = zero

    @pl.when(n % nj == 0)
    def _():
        t = jnp.minimum(n // nj, n_tiles - 1) % tiles_per_seq
        zero = jnp.zeros((HALO, lhs_ref.shape[1]), BF16)
        lhs_ref[0:HALO] = jnp.where(t == 0, zero, hp_ref[...])
        lhs_ref[HALO:HALO + tm] = hc_ref[...]
        lhs_ref[HALO + tm:HALO + tm + HALO] = jnp.where(t == tiles_per_seq - 1, zero, hn_ref[...])

    tn = o_ref.shape[1]
    half = tn // 2

    def conv(u_ref, prv, cw_ref, cb_ref, lanes):
        cw = cw_ref[:, lanes]
        return (u_ref[prv, HALO - 1:HALO - 1 + tm, lanes] * cw[0:1]
                + u_ref[prv, HALO:HALO + tm, lanes] * cw[1:2]
                + u_ref[prv, HALO + 1:HALO + 1 + tm, lanes] * cw[2:3]
                + cb_ref[:, lanes])

    def epilogue(prv, c):
        lanes = slice(c * LANES, (c + 1) * LANES)
        gate = conv(ug_ref, prv, cwg_ref, cbg_ref, lanes)
        val = conv(uv_ref, prv, cwv_ref, cbv_ref, lanes)
        o_ref[:, lanes] = (gate * (1.0 / (1.0 + jnp.exp(-gate))) * val).astype(BF16)

    def body(cur, prv):
        lhs = lhs_ref[...]
        n_chunks = tn // LANES
        lo = slice(0, half)
        hi = slice(half, tn)
        ug_ref[cur, :, lo] = jnp.dot(lhs, wg_ref[:, lo], preferred_element_type=F32)
        for c in range(0, n_chunks // 4):
            epilogue(prv, c)
        uv_ref[cur, :, lo] = jnp.dot(lhs, wv_ref[:, lo], preferred_element_type=F32)
        for c in range(n_chunks // 4, n_chunks // 2):
            epilogue(prv, c)
        ug_ref[cur, :, hi] = jnp.dot(lhs, wg_ref[:, hi], preferred_element_type=F32)
        for c in range(n_chunks // 2, n_chunks):
            epilogue(prv, c)
        uv_ref[cur, :, hi] = jnp.dot(lhs, wv_ref[:, hi], preferred_element_type=F32)

    @pl.when(slot == 0)
    def _():
        body(0, 1)

    @pl.when(slot == 1)
    def _():
        body(1, 0)


def _up_proj(h2, w_up, conv_w, conv_b, *, seq, tm=512, tn=512):
    ntok, d_model = h2.shape
    d_ff = w_up.shape[1] // 2
    nj = d_ff // tn
    n_tiles = ntok // tm
    tps = seq // tm
    hb = tm // HALO
    last = ntok // HALO - 1
    kern = functools.partial(_up_kernel, tm=tm, nj=nj, n_tiles=n_tiles, tiles_per_seq=tps)

    def tile(n):
        return jnp.minimum(n // nj, n_tiles - 1)

    def pj(n):
        return jnp.maximum(n - 1, 0) % nj

    return pl.pallas_call(
        kern,
        grid=(n_tiles * nj + 1,),
        in_specs=[
            pl.BlockSpec((HALO, d_model), lambda n: (jnp.maximum(tile(n) * hb - 1, 0), 0)),
            pl.BlockSpec((tm, d_model), lambda n: (tile(n), 0)),
            pl.BlockSpec((HALO, d_model), lambda n: (jnp.minimum((tile(n) + 1) * hb, last), 0)),
            pl.BlockSpec((d_model, tn), lambda n: (0, n % nj)),
            pl.BlockSpec((d_model, tn), lambda n: (0, nj + n % nj)),
            pl.BlockSpec((3, tn), lambda n: (0, pj(n))),
            pl.BlockSpec((3, tn), lambda n: (0, nj + pj(n))),
            pl.BlockSpec((1, tn), lambda n: (0, pj(n))),
            pl.BlockSpec((1, tn), lambda n: (0, nj + pj(n))),
        ],
        out_specs=pl.BlockSpec((tm, tn), lambda n: (jnp.maximum(n - 1, 0) // nj, pj(n))),
        out_shape=jax.ShapeDtypeStruct((ntok, d_ff), BF16),
        scratch_shapes=[pltpu.VMEM((tm + 2 * HALO, d_model), BF16),
                        pltpu.VMEM((2, tm + 2 * HALO, tn), F32),
                        pltpu.VMEM((2, tm + 2 * HALO, tn), F32)],
        compiler_params=_cparams(("arbitrary",)),
        name="up_proj",
    )(h2, h2, h2, w_up, w_up, conv_w, conv_w, conv_b, conv_b)


def _down_kernel(a_ref, w_ref, x1_ref, g_ref, y_ref, *, tn, n_blocks):
    n = pl.program_id(1)
    y = jnp.dot(a_ref[...], w_ref[...], preferred_element_type=F32)
    for b in range(n_blocks):
        @pl.when(n == b)
        def _(b=b):
            cols = slice(b * tn, (b + 1) * tn)
            y_ref[:, cols] = x1_ref[:, cols] + y

    @pl.when(n == n_blocks - 1)
    def _():
        y_ref[...] = _rms(y_ref[...], g_ref[...])


def _down_proj(act, w_down, x1, final_g, *, tm=512, tn=512):
    ntok, d_ff = act.shape
    d_model = w_down.shape[1]
    n_blocks = d_model // tn
    kern = functools.partial(_down_kernel, tn=tn, n_blocks=n_blocks)
    return pl.pallas_call(
        kern,
        grid=(ntok // tm, n_blocks),
        in_specs=[
            pl.BlockSpec((tm, d_ff), lambda i, n: (i, 0)),
            pl.BlockSpec((d_ff, tn), lambda i, n: (0, n)),
            pl.BlockSpec((tm, d_model), lambda i, n: (i, 0)),
            pl.BlockSpec((1, d_model), lambda i, n: (0, 0)),
        ],
        out_specs=pl.BlockSpec((tm, d_model), lambda i, n: (i, 0)),
        out_shape=jax.ShapeDtypeStruct((ntok, d_model), F32),
        compiler_params=_cparams(("parallel", "arbitrary")),
        name="down_proj",
    )(act, w_down, x1, final_g)


def _rope_lane_tables(seq):
    inv_freq = ROPE_THETA ** (-jnp.arange(0, ROPE_DIM, 2, dtype=F32) / ROPE_DIM)
    ang = jnp.arange(seq, dtype=F32)[:, None] * inv_freq[None, :]
    cos, sin = jnp.cos(ang), jnp.sin(ang)
    half = ROPE_DIM // 2
    rest = HEAD_DIM - ROPE_DIM
    cos_h = jnp.concatenate([cos, cos, jnp.ones((seq, rest), F32)], axis=1)
    sa_h = jnp.concatenate([-sin, jnp.zeros((seq, HEAD_DIM - half), F32)], axis=1)
    sb_h = jnp.concatenate([jnp.zeros((seq, half), F32), sin, jnp.zeros((seq, rest), F32)], axis=1)
    reps = LANES // HEAD_DIM
    rope = jnp.stack([jnp.tile(t, (1, reps)) for t in (cos_h, sa_h, sb_h)])
    ident = jnp.stack([jnp.ones((seq, LANES), F32), jnp.zeros((seq, LANES), F32),
                       jnp.zeros((seq, LANES), F32)])
    return jnp.stack([rope, ident])


def _encoder(x, p):
    batch, seq, d_model = x.shape
    aw = p["attn_width"]
    ntok = batch * seq
    x2 = x.reshape(ntok, d_model)

    tn = N_FOURIER_GROUPS * FOURIER_GROUP
    nat, c4, c16 = _in_proj(x2, p["norm1_g"], p["w_in"], p["rope"], batch=batch, seq=seq,
                            n_rope_tiles=2 * aw // tn, tn=tn)
    nat3 = nat.reshape(batch, seq, -1)
    o1, l1 = _attention(nat3.reshape(batch, 1, seq, -1), attn_width=aw)
    o4, l4 = _attention(c4, attn_width=aw)
    o16, l16 = _attention(c16, attn_width=aw)
    four = _fourier(nat3, first_lane_block=3 * aw // FOURIER_GROUP)

    x1, h2 = _out_proj(o1.reshape(ntok, aw), o4, o16, l1.reshape(ntok, LANES), l4, l16,
                       four.reshape(ntok, -1), x2,
                       p["attn_out_g"], p["fourier_out_g"], p["w_out_a"], p["w_out_f"],
                       p["norm2_g"], seq=seq)
    act = _up_proj(h2, p["w_up"], p["conv_w"], p["conv_b"], seq=seq)
    y = _down_proj(act, p["w_down"], x1, p["final_g"])
    return y.reshape(batch, seq, d_model)


def _prepare(seq, norm1_g, w_in, attn_out_g, fourier_out_g, w_out, norm2_g, w_up, conv_w, conv_b,
             w_down, final_g):
    aw = (w_in.shape[1] - N_FOURIER_GROUPS * FOURIER_GROUP) // 3
    col_scale = jnp.where(jnp.arange(w_in.shape[1]) < aw, LOG2E / math.sqrt(HEAD_DIM), 1.0)
    w_out_b = w_out.astype(BF16)
    return {
        "attn_width": aw,
        "rope": _rope_lane_tables(seq),
        "norm1_g": norm1_g.reshape(1, -1),
        "w_in": (w_in * col_scale[None, :].astype(F32)).astype(BF16),
        "attn_out_g": attn_out_g.reshape(1, -1),
        "fourier_out_g": fourier_out_g.reshape(1, -1),
        "w_out_a": w_out_b[:aw],
        "w_out_f": w_out_b[aw:],
        "norm2_g": norm2_g.reshape(1, -1),
        "w_up": w_up.astype(BF16),
        "conv_w": conv_w,
        "conv_b": conv_b.reshape(1, -1),
        "w_down": w_down.astype(BF16),
        "final_g": final_g.reshape(1, -1),
    }


def kernel(x_prompt, x_sample, norm1_g, w_in, attn_out_g, fourier_out_g, w_out, norm2_g, w_up,
           conv_w, conv_b, w_down, final_g):
    assert norm1_g.shape[0] == 1, "single-layer encoder"
    assert x_prompt.shape[1] == x_sample.shape[1]
    p = _prepare(x_prompt.shape[1], norm1_g[0], w_in[0], attn_out_g[0], fourier_out_g[0], w_out[0],
                 norm2_g[0], w_up[0], conv_w[0], conv_b[0], w_down[0], final_g)
    return (_encoder(x_prompt, p), _encoder(x_sample, p))
```

```python
import functools
import math

import numpy as np
import jax
import jax.numpy as jnp
from jax import lax
from jax.experimental import pallas as pl
from jax.experimental.pallas import tpu as pltpu

F32 = jnp.float32
BF16 = jnp.bfloat16

HEAD_DIM = 64
ROPE_DIM = 16
ROPE_THETA = 500000.0
N_FOURIER_GROUPS = 4
FOURIER_GROUP = 128
DILATIONS = (1, 4, 16)
BAND = 64
EPS = 1e-6
MASK_VALUE = -1e30
LOG2E = 1.4426950408889634
LN2 = 0.6931471805599453

LANES = 128
VMEM_LIMIT_BYTES = 56 * 1024 * 1024

DFT_N1 = 64
DFT_PAD = 8


def _cparams(semantics):
    return pltpu.CompilerParams(dimension_semantics=semantics,
                                vmem_limit_bytes=VMEM_LIMIT_BYTES)


def _rms(x, g):
    var = jnp.mean(x * x, axis=-1, keepdims=True)
    return x * lax.rsqrt(var + EPS) * g


def _in_proj_kernel(x_ref, g_ref, w_ref, tab_ref, nat_ref, c4_ref, c16_ref,
                    h_ref, acc_ref, s1_ref, s2_ref, *, nj):
    n = pl.program_id(0)
    slot = n % 2
    tm = acc_ref.shape[1]
    n_chunks = acc_ref.shape[2] // LANES

    @pl.when(n == 0)
    def _():
        acc_ref[1] = jnp.zeros(acc_ref.shape[1:], F32)

    @pl.when(n % nj == 0)
    def _():
        h_ref[...] = _rms(x_ref[...], g_ref[...]).astype(BF16)

    cos = tab_ref[0]
    sa = tab_ref[1]
    sb = tab_ref[2]
    for c in range(n_chunks):
        lanes = slice(c * LANES, (c + 1) * LANES)
        a = acc_ref[1 - slot, :, lanes]
        r = a * cos + pltpu.roll(a, LANES - ROPE_DIM // 2, 1) * sa \
            + pltpu.roll(a, ROPE_DIM // 2, 1) * sb
        nat_ref[:, lanes] = r.astype(BF16)
        s1_ref[c] = r
        for r4 in range(4):
            x4 = s1_ref[c, pl.ds(r4, tm // 4, stride=4), :]
            c4_ref[r4, :, lanes] = x4.astype(BF16)
            s2_ref[c, r4] = x4
            for q4 in range(4):
                x16 = s2_ref[c, r4, pl.ds(q4, tm // 16, stride=4), :]
                c16_ref[r4 + 4 * q4, :, lanes] = x16.astype(BF16)

    acc_ref[slot] = jnp.dot(h_ref[...], w_ref[...], preferred_element_type=F32)


def _in_proj(x2, g, w, tabs, *, batch, seq, n_rope_tiles, tm=512, tn=512):
    ntok, d_model = x2.shape
    width = w.shape[1]
    nj = width // tn
    ntiles = ntok // tm
    assert tm % 256 == 0
    tps = seq // tm
    kern = functools.partial(_in_proj_kernel, nj=nj)

    def prev(n):
        pn = jnp.maximum(n - 1, 0)
        return pn // nj, pn % nj

    def tab_map(n):
        pi, pj = prev(n)
        return (jnp.where(pj < n_rope_tiles, 0, 1), 0, pi % tps, 0)

    def nat_map(n):
        return prev(n)

    def cls_map(n):
        pi, pj = prev(n)
        return (pi // tps, 0, pi % tps, pj)

    return pl.pallas_call(
        kern,
        grid=(ntiles * nj + 1,),
        in_specs=[
            pl.BlockSpec((tm, d_model), lambda n: (jnp.minimum(n // nj, ntiles - 1), 0)),
            pl.BlockSpec((1, d_model), lambda n: (0, 0)),
            pl.BlockSpec((d_model, tn), lambda n: (0, n % nj)),
            pl.BlockSpec((None, 3, tm, LANES), tab_map),
        ],
        out_specs=[
            pl.BlockSpec((tm, tn), nat_map),
            pl.BlockSpec((None, 4, tm // 4, tn), cls_map),
            pl.BlockSpec((None, 16, tm // 16, tn), cls_map),
        ],
        out_shape=[
            jax.ShapeDtypeStruct((ntok, width), BF16),
            jax.ShapeDtypeStruct((batch, 4, seq // 4, width), BF16),
            jax.ShapeDtypeStruct((batch, 16, seq // 16, width), BF16),
        ],
        scratch_shapes=[pltpu.VMEM((tm, d_model), BF16),
                        pltpu.VMEM((2, tm, tn), F32),
                        pltpu.VMEM((tn // LANES, tm, LANES), F32),
                        pltpu.VMEM((tn // LANES, 4, tm // 4, LANES), F32)],
        compiler_params=_cparams(("arbitrary",)),
        name="in_proj",
    )(x2, g, w, tabs)


def _attn_kernel(bias_ref, q_ref, kp_ref, kc_ref, kn_ref, vp_ref, vc_ref, vn_ref,
                 o_ref, lse_ref, v0_ref, v1_ref, s_ref, p_ref, m_ref,
                 *, tb, tq, n_chunks, pairs, row_chunk=32):
    j = pl.program_id(2)
    tk = tq + 2 * BAND
    n_sub = tb // tq

    lane = lax.broadcasted_iota(jnp.int32, (tq, LANES), 1)
    first_head = lane < HEAD_DIM
    first_chunk = lax.broadcasted_iota(jnp.int32, (row_chunk, LANES), 1) < HEAD_DIM
    first_w = lax.broadcasted_iota(jnp.int32, (tk, LANES), 1) < HEAD_DIM
    ones = jnp.ones((tk, LANES), BF16)

    def window(prev_ref, cur_ref, next_ref, i, lanes):
        if i == 0:
            return jnp.concatenate([prev_ref[:, lanes], cur_ref[0:tq + BAND, lanes]], axis=0)
        if i == n_sub - 1:
            return jnp.concatenate([cur_ref[tb - tq - BAND:tb, lanes], next_ref[:, lanes]], axis=0)
        return cur_ref[i * tq - BAND:(i + 1) * tq + BAND, lanes]

    units = [(i, p) for i in range(n_sub) for p in range(pairs)]

    def stage_a(u):
        i, p = units[u]
        slot = u % 2
        lanes = slice(p * LANES, (p + 1) * LANES)
        vwin = window(vp_ref, vc_ref, vn_ref, i, lanes)
        v0_ref[u % 3] = jnp.where(first_w, vwin, ones)
        v1_ref[u % 3] = jnp.where(first_w, ones, vwin)
        q = q_ref[i * tq:(i + 1) * tq, lanes]
        zero = jnp.zeros_like(q)
        q2 = jnp.concatenate([jnp.where(first_head, q, zero), jnp.where(first_head, zero, q)],
                             axis=0)
        kw = window(kp_ref, kc_ref, kn_ref, i, lanes)
        s = lax.dot_general(q2, kw, (((1,), (1,)), ((), ())), preferred_element_type=F32)
        if i == 0:
            idx = jnp.where(j == 0, 1, 0)
        elif i == n_sub - 1:
            idx = jnp.where(j == n_chunks - 1, 2, 0)
        else:
            idx = 0
        s_ref[slot] = s + bias_ref[idx]

    def stage_b(u):
        slot = u % 2
        for r in range(0, tq, row_chunk):
            s0 = s_ref[slot, r:r + row_chunk, :]
            s1 = s_ref[slot, tq + r:tq + r + row_chunk, :]
            m0 = jnp.max(s0, axis=-1, keepdims=True)
            m1 = jnp.max(s1, axis=-1, keepdims=True)
            p_ref[slot, r:r + row_chunk, :] = jnp.exp2(s0 - m0).astype(BF16)
            p_ref[slot, tq + r:tq + r + row_chunk, :] = jnp.exp2(s1 - m1).astype(BF16)
            m_ref[slot, r:r + row_chunk, :] = jnp.where(
                first_chunk, jnp.broadcast_to(m0, (row_chunk, LANES)),
                jnp.broadcast_to(m1, (row_chunk, LANES)))

    def stage_c(u):
        i, p = units[u]
        slot = u % 2
        lanes = slice(p * LANES, (p + 1) * LANES)
        pv0 = jnp.dot(p_ref[slot, 0:tq, :], v0_ref[u % 3], preferred_element_type=F32)
        pv1 = jnp.dot(p_ref[slot, tq:2 * tq, :], v1_ref[u % 3], preferred_element_type=F32)
        o = jnp.where(first_head, pv0, pv1)
        dens = pltpu.roll(jnp.where(first_head, pv1, pv0), HEAD_DIM, 1)
        o_ref[i * tq:(i + 1) * tq, lanes] = (o * (1.0 / dens)).astype(BF16)
        lse = m_ref[slot] * LN2 + jnp.log(dens)
        cur = jnp.zeros((tq, LANES), F32) if p == 0 else lse_ref[i * tq:(i + 1) * tq, :]
        lse_ref[i * tq:(i + 1) * tq, :] = jnp.where(
            jnp.bitwise_and(lane, HEAD_DIM - 1) == p, lse, cur)

    n_units = len(units)
    for t in range(n_units + 2):
        if t < n_units:
            stage_a(t)
        if 0 <= t - 1 < n_units:
            stage_b(t - 1)
        if 0 <= t - 2 < n_units:
            stage_c(t - 2)


def _attn_bias(tq):
    tk = tq + 2 * BAND
    col = np.arange(tk)[None, :]
    row = np.arange(tq)[:, None]
    band = (col - row >= 0) & (col - row <= 2 * BAND)
    variants = [band, band & (col >= BAND), band & (col < tq + BAND)]
    bias = np.stack([np.where(v, 0.0, MASK_VALUE) for v in variants]).astype(np.float32)
    return jnp.asarray(np.concatenate([bias, bias], axis=1))


def _attention(qkv, *, attn_width, tq=128, max_tb=512):
    batch, d, t_len, width = qkv.shape
    tb = min(max_tb, t_len)
    assert t_len % tb == 0 and tb % tq == 0 and tb // tq >= 2
    n_chunks = t_len // tb
    pairs = attn_width // LANES
    hb = tb // BAND
    last_hb = t_len // BAND - 1
    tk = tq + 2 * BAND

    def cur(off):
        return pl.BlockSpec((None, None, tb, attn_width), lambda b, r, j: (b, r, j, off))

    def prv(off):
        return pl.BlockSpec((None, None, BAND, attn_width),
                            lambda b, r, j: (b, r, jnp.maximum(j * hb - 1, 0), off))

    def nxt(off):
        return pl.BlockSpec((None, None, BAND, attn_width),
                            lambda b, r, j: (b, r, jnp.minimum((j + 1) * hb, last_hb), off))

    kern = functools.partial(_attn_kernel, tb=tb, tq=tq, n_chunks=n_chunks, pairs=pairs)
    return pl.pallas_call(
        kern,
        grid=(batch, d, n_chunks),
        in_specs=[pl.BlockSpec((3, 2 * tq, tk), lambda b, r, j: (0, 0, 0)),
                  cur(0), prv(1), cur(1), nxt(1), prv(2), cur(2), nxt(2)],
        out_specs=[
            pl.BlockSpec((None, None, tb, attn_width), lambda b, r, j: (b, r, j, 0)),
            pl.BlockSpec((None, None, tb, LANES), lambda b, r, j: (b, r, j, 0)),
        ],
        out_shape=[
            jax.ShapeDtypeStruct((batch, d, t_len, attn_width), BF16),
            jax.ShapeDtypeStruct((batch, d, t_len, LANES), F32),
        ],
        scratch_shapes=[pltpu.VMEM((3, tk, LANES), BF16),
                        pltpu.VMEM((3, tk, LANES), BF16),
                        pltpu.VMEM((2, 2 * tq, tk), F32),
                        pltpu.VMEM((2, 2 * tq, tk), BF16),
                        pltpu.VMEM((2, tq, LANES), F32)],
        compiler_params=_cparams(("parallel", "parallel", "parallel")),
        name=f"attn_d{d}",
    )(_attn_bias(tq), qkv, qkv, qkv, qkv, qkv, qkv, qkv)


def _fourier_kernel(x_ref, m1_ref, m2_ref, csc_ref, o_ref, xf_ref, yr_ref, yi_ref,
                    *, n1, n2, pitch, scale):
    xf_ref[...] = x_ref[...].astype(F32)

    def stage1(m, carry):
        xm = xf_ref[pl.ds(m, n1, stride=n2), :].astype(BF16)
        y = jnp.dot(m1_ref[m], xm, preferred_element_type=F32)
        yr_ref[pl.ds(m, n1, stride=pitch), :] = y[:n1]
        yi_ref[pl.ds(m, n1, stride=pitch), :] = y[n1:]
        return carry

    lax.fori_loop(0, n2, stage1, 0, unroll=8)

    m2 = m2_ref[...]
    csc = csc_ref[...]

    def stage2(kk, carry):
        ka = 2 * kk
        kb = ka + 1
        base_a = pl.multiple_of(ka * pitch, 8)
        base_b = pl.multiple_of(kb * pitch, 8)
        dat_a = jnp.concatenate([yr_ref[pl.ds(base_a, n2), :], yi_ref[pl.ds(base_a, n2), :]], axis=0)
        dat_b = jnp.concatenate([yr_ref[pl.ds(base_b, n2), :], yi_ref[pl.ds(base_b, n2), :]], axis=0)
        dat = jnp.concatenate([dat_a, dat_b], axis=1).astype(BF16)
        o = jnp.dot(m2, dat, preferred_element_type=F32).astype(BF16)
        lhs = jnp.concatenate(
            [jnp.concatenate([o[:n2, :LANES], o[n2:, :LANES]], axis=1),
             jnp.concatenate([o[:n2, LANES:], o[n2:, LANES:]], axis=1)], axis=0)
        res = jnp.dot(lhs, csc, preferred_element_type=F32) * scale
        o_ref[pl.ds(ka, n2, stride=n1), :] = res[:n2]
        o_ref[pl.ds(kb, n2, stride=n1), :] = res[n2:]
        return carry

    lax.fori_loop(0, n1 // 2, stage2, 0, unroll=4)


def _dft_mats(n):
    k = np.arange(n)
    ang = 2.0 * np.pi * ((k[:, None] * k[None, :]) % n) / n
    return np.cos(ang), np.sin(ang)


def _fourier(f3, *, first_lane_block=0):
    batch, seq, _ = f3.shape
    ch = FOURIER_GROUP
    groups = N_FOURIER_GROUPS
    width = groups * ch
    n1 = DFT_N1
    n2 = seq // n1
    pitch = n2 + DFT_PAD
    pos = n2 * np.arange(n1)[None, None, :] + np.arange(n2)[:, None, None]
    ang = 2.0 * np.pi * ((np.arange(n1)[None, :, None] * pos) % seq) / seq
    m1 = jnp.asarray(np.concatenate([np.cos(ang), -np.sin(ang)], axis=1), F32).astype(BF16)
    c2, s2 = _dft_mats(n2)
    m2 = jnp.asarray(np.block([[c2, s2], [-s2, c2]]), F32).astype(BF16)
    cch, sch = _dft_mats(ch)
    csc = jnp.asarray(np.concatenate([cch, sch], axis=0), F32).astype(BF16)
    assert n1 % 2 == 0
    kern = functools.partial(_fourier_kernel, n1=n1, n2=n2, pitch=pitch,
                             scale=1.0 / math.sqrt(seq * ch))
    return pl.pallas_call(
        kern,
        grid=(batch, groups),
        in_specs=[
            pl.BlockSpec((None, seq, ch), lambda b, g: (b, 0, first_lane_block + g)),
            pl.BlockSpec((n2, 2 * n1, n1), lambda b, g: (0, 0, 0)),
            pl.BlockSpec((2 * n2, 2 * n2), lambda b, g: (0, 0)),
            pl.BlockSpec((2 * ch, ch), lambda b, g: (0, 0)),
        ],
        out_specs=pl.BlockSpec((None, seq, ch), lambda b, g: (b, 0, g)),
        out_shape=jax.ShapeDtypeStruct((batch, seq, width), F32),
        scratch_shapes=[pltpu.VMEM((seq, ch), F32),
                        pltpu.VMEM((n1 * pitch, ch), F32),
                        pltpu.VMEM((n1 * pitch, ch), F32)],
        compiler_params=_cparams(("parallel", "parallel")),
        name="fourier",
    )(f3, m1, m2, csc)


def _out_proj_kernel(o1_ref, o4_ref, o16_ref, l1_ref, l4_ref, l16_ref, four_ref, x_ref,
                     ga_ref, gf_ref, wa_ref, wf_ref, g2_ref, e_ref, x1_ref, h2_ref,
                     n4_ref, n16_ref, ln4_ref, ln16_ref):
    tm = x_ref.shape[0]
    n_slabs = n4_ref.shape[0]

    for r in range(4):
        blk = o4_ref[r].astype(F32)
        for c in range(n_slabs):
            n4_ref[c, pl.ds(r, tm // 4, stride=4), :] = blk[:, c * LANES:(c + 1) * LANES]
        ln4_ref[pl.ds(r, tm // 4, stride=4), :] = l4_ref[r]
    for r in range(16):
        blk = o16_ref[r].astype(F32)
        for c in range(n_slabs):
            n16_ref[c, pl.ds(r, tm // 16, stride=16), :] = blk[:, c * LANES:(c + 1) * LANES]
        ln16_ref[pl.ds(r, tm // 16, stride=16), :] = l16_ref[r]

    l1 = l1_ref[...]
    l2 = ln4_ref[...]
    l3 = ln16_ref[...]
    m = jnp.maximum(jnp.maximum(l1, l2), l3)
    e1 = jnp.exp(l1 - m)
    e2 = jnp.exp(l2 - m)
    e3 = jnp.exp(l3 - m)
    inv = 1.0 / (e1 + e2 + e3)
    expand = e_ref[...]

    def spread(w):
        hi = w.astype(BF16)
        lo = (w - hi.astype(F32)).astype(BF16)
        return jnp.dot(hi, expand, preferred_element_type=F32) \
            + jnp.dot(lo, expand, preferred_element_type=F32)

    w1 = spread(e1 * inv)
    w2 = spread(e2 * inv)
    o1 = o1_ref[...].astype(F32)
    slabs = []
    ssq = jnp.zeros((tm, 1), F32)
    for c in range(n_slabs):
        lanes = slice(c * LANES, (c + 1) * LANES)
        o3 = n16_ref[c]
        a = o3 + w1[:, lanes] * (o1[:, lanes] - o3) + w2[:, lanes] * (n4_ref[c] - o3)
        ssq = ssq + jnp.sum(a * a, axis=-1, keepdims=True)
        slabs.append(a)
    attn = jnp.concatenate(slabs, axis=1)
    width = n_slabs * LANES
    a_n = (attn * lax.rsqrt(ssq / width + EPS) * ga_ref[...]).astype(BF16)
    f_n = _rms(four_ref[...], gf_ref[...]).astype(BF16)
    y = jnp.dot(a_n, wa_ref[...], preferred_element_type=F32) \
        + jnp.dot(f_n, wf_ref[...], preferred_element_type=F32)
    x1 = x_ref[...] + y
    x1_ref[...] = x1
    h2_ref[...] = _rms(x1, g2_ref[...]).astype(BF16)


def _out_proj(o1, o4, o16, l1, l4, l16, four, x2, ga, gf, wa, wf, g2, *, seq, tm=256):
    ntok, d_model = x2.shape
    aw = o1.shape[1]
    fw = four.shape[1]
    n_heads = aw // HEAD_DIM
    tps = seq // tm
    expand = np.zeros((LANES, aw), np.float32)
    for h in range(n_heads):
        expand[(h % 2) * HEAD_DIM + h // 2, h * HEAD_DIM:(h + 1) * HEAD_DIM] = 1.0
    expand = jnp.asarray(expand).astype(BF16)
    row = lambda w: pl.BlockSpec((tm, w), lambda i: (i, 0))
    cls = lambda d, w: pl.BlockSpec((None, d, tm // d, w), lambda i: (i // tps, 0, i % tps, 0))
    const = lambda a, b: pl.BlockSpec((a, b), lambda i: (0, 0))
    return pl.pallas_call(
        _out_proj_kernel,
        grid=(ntok // tm,),
        in_specs=[row(aw), cls(4, aw), cls(16, aw), row(LANES), cls(4, LANES), cls(16, LANES),
                  row(fw), row(d_model),
                  const(1, aw), const(1, fw), const(aw, d_model), const(fw, d_model),
                  const(1, d_model), const(LANES, aw)],
        out_specs=[row(d_model), row(d_model)],
        out_shape=[jax.ShapeDtypeStruct((ntok, d_model), F32),
                   jax.ShapeDtypeStruct((ntok, d_model), BF16)],
        scratch_shapes=[pltpu.VMEM((aw // LANES, tm, LANES), F32),
                        pltpu.VMEM((aw // LANES, tm, LANES), F32),
                        pltpu.VMEM((tm, LANES), F32),
                        pltpu.VMEM((tm, LANES), F32)],
        compiler_params=_cparams(("parallel",)),
        name="out_proj",
    )(o1, o4, o16, l1, l4, l16, four, x2, ga, gf, wa, wf, g2, expand)


HALO = 16


def _up_kernel(hp_ref, hc_ref, hn_ref, wg_ref, wv_ref, cwg_ref, cwv_ref, cbg_ref, cbv_ref,
               o_ref, lhs_ref, ug_ref, uv_ref, *, tm, tiles_per_seq):
    i = pl.program_id(0)
    j = pl.program_id(1)

    @pl.when(j == 0)
    def _():
        t = i % tiles_per_seq
        zero = jnp.zeros((HALO, lhs_ref.shape[1]), BF16)
        lhs_ref[0:HALO] = jnp.where(t == 0, zero, hp_ref[...])
        lhs_ref[HALO:HALO + tm] = hc_ref[...]
        lhs_ref[HALO + tm:HALO + tm + HALO] = jnp.where(t == tiles_per_seq - 1, zero, hn_ref[...])

    lhs = lhs_ref[...]
    ug_ref[...] = jnp.dot(lhs, wg_ref[...], preferred_element_type=F32)
    uv_ref[...] = jnp.dot(lhs, wv_ref[...], preferred_element_type=F32)

    def conv(u_ref, cw_ref, cb_ref):
        cw = cw_ref[...]
        return (u_ref[HALO - 1:HALO - 1 + tm] * cw[0:1]
                + u_ref[HALO:HALO + tm] * cw[1:2]
                + u_ref[HALO + 1:HALO + 1 + tm] * cw[2:3]
                + cb_ref[...])

    gate = conv(ug_ref, cwg_ref, cbg_ref)
    val = conv(uv_ref, cwv_ref, cbv_ref)
    o_ref[...] = (gate * (1.0 / (1.0 + jnp.exp(-gate))) * val).astype(BF16)


def _up_proj(h2, w_up, conv_w, conv_b, *, seq, tm=512, tn=512):
    ntok, d_model = h2.shape
    d_ff = w_up.shape[1] // 2
    nj = d_ff // tn
    tps = seq // tm
    hb = tm // HALO
    last = ntok // HALO - 1
    kern = functools.partial(_up_kernel, tm=tm, tiles_per_seq=tps)
    return pl.pallas_call(
        kern,
        grid=(ntok // tm, nj),
        in_specs=[
            pl.BlockSpec((HALO, d_model), lambda i, j: (jnp.maximum(i * hb - 1, 0), 0)),
            pl.BlockSpec((tm, d_model), lambda i, j: (i, 0)),
            pl.BlockSpec((HALO, d_model), lambda i, j: (jnp.minimum((i + 1) * hb, last), 0)),
            pl.BlockSpec((d_model, tn), lambda i, j: (0, j)),
            pl.BlockSpec((d_model, tn), lambda i, j: (0, nj + j)),
            pl.BlockSpec((3, tn), lambda i, j: (0, j)),
            pl.BlockSpec((3, tn), lambda i, j: (0, nj + j)),
            pl.BlockSpec((1, tn), lambda i, j: (0, j)),
            pl.BlockSpec((1, tn), lambda i, j: (0, nj + j)),
        ],
        out_specs=pl.BlockSpec((tm, tn), lambda i, j: (i, j)),
        out_shape=jax.ShapeDtypeStruct((ntok, d_ff), BF16),
        scratch_shapes=[pltpu.VMEM((tm + 2 * HALO, d_model), BF16),
                        pltpu.VMEM((tm + 2 * HALO, tn), F32),
                        pltpu.VMEM((tm + 2 * HALO, tn), F32)],
        compiler_params=_cparams(("parallel", "arbitrary")),
        name="up_proj",
    )(h2, h2, h2, w_up, w_up, conv_w, conv_w, conv_b, conv_b)


def _down_kernel(a_ref, w_ref, x1_ref, g_ref, y_ref, *, tn, n_blocks):
    n = pl.program_id(1)
    y = jnp.dot(a_ref[...], w_ref[...], preferred_element_type=F32)
    for b in range(n_blocks):
        @pl.when(n == b)
        def _(b=b):
            cols = slice(b * tn, (b + 1) * tn)
            y_ref[:, cols] = x1_ref[:, cols] + y

    @pl.when(n == n_blocks - 1)
    def _():
        y_ref[...] = _rms(y_ref[...], g_ref[...])


def _down_proj(act, w_down, x1, final_g, *, tm=512, tn=512):
    ntok, d_ff = act.shape
    d_model = w_down.shape[1]
    n_blocks = d_model // tn
    kern = functools.partial(_down_kernel, tn=tn, n_blocks=n_blocks)
    return pl.pallas_call(
        kern,
        grid=(ntok // tm, n_blocks),
        in_specs=[
            pl.BlockSpec((tm, d_ff), lambda i, n: (i, 0)),
            pl.BlockSpec((d_ff, tn), lambda i, n: (0, n)),
            pl.BlockSpec((tm, d_model), lambda i, n: (i, 0)),
            pl.BlockSpec((1, d_model), lambda i, n: (0, 0)),
        ],
        out_specs=pl.BlockSpec((tm, d_model), lambda i, n: (i, 0)),
        out_shape=jax.ShapeDtypeStruct((ntok, d_model), F32),
        compiler_params=_cparams(("parallel", "arbitrary")),
        name="down_proj",
    )(act, w_down, x1, final_g)


def _rope_lane_tables(seq):
    inv_freq = ROPE_THETA ** (-jnp.arange(0, ROPE_DIM, 2, dtype=F32) / ROPE_DIM)
    ang = jnp.arange(seq, dtype=F32)[:, None] * inv_freq[None, :]
    cos, sin = jnp.cos(ang), jnp.sin(ang)
    half = ROPE_DIM // 2
    rest = HEAD_DIM - ROPE_DIM
    cos_h = jnp.concatenate([cos, cos, jnp.ones((seq, rest), F32)], axis=1)
    sa_h = jnp.concatenate([-sin, jnp.zeros((seq, HEAD_DIM - half), F32)], axis=1)
    sb_h = jnp.concatenate([jnp.zeros((seq, half), F32), sin, jnp.zeros((seq, rest), F32)], axis=1)
    reps = LANES // HEAD_DIM
    rope = jnp.stack([jnp.tile(t, (1, reps)) for t in (cos_h, sa_h, sb_h)])
    ident = jnp.stack([jnp.ones((seq, LANES), F32), jnp.zeros((seq, LANES), F32),
                       jnp.zeros((seq, LANES), F32)])
    return jnp.stack([rope, ident])


def _encoder(x, p):
    batch, seq, d_model = x.shape
    aw = p["attn_width"]
    ntok = batch * seq
    x2 = x.reshape(ntok, d_model)

    tn = N_FOURIER_GROUPS * FOURIER_GROUP
    nat, c4, c16 = _in_proj(x2, p["norm1_g"], p["w_in"], p["rope"], batch=batch, seq=seq,
                            n_rope_tiles=2 * aw // tn, tn=tn)
    nat3 = nat.reshape(batch, seq, -1)
    o1, l1 = _attention(nat3.reshape(batch, 1, seq, -1), attn_width=aw)
    o4, l4 = _attention(c4, attn_width=aw)
    o16, l16 = _attention(c16, attn_width=aw)
    four = _fourier(nat3, first_lane_block=3 * aw // FOURIER_GROUP)

    x1, h2 = _out_proj(o1.reshape(ntok, aw), o4, o16, l1.reshape(ntok, LANES), l4, l16,
                       four.reshape(ntok, -1), x2,
                       p["attn_out_g"], p["fourier_out_g"], p["w_out_a"], p["w_out_f"],
                       p["norm2_g"], seq=seq)
    act = _up_proj(h2, p["w_up"], p["conv_w"], p["conv_b"], seq=seq)
    y = _down_proj(act, p["w_down"], x1, p["final_g"])
    return y.reshape(batch, seq, d_model)


def _prepare(seq, norm1_g, w_in, attn_out_g, fourier_out_g, w_out, norm2_g, w_up, conv_w, conv_b,
             w_down, final_g):
    aw = (w_in.shape[1] - N_FOURIER_GROUPS * FOURIER_GROUP) // 3
    col_scale = jnp.where(jnp.arange(w_in.shape[1]) < aw, LOG2E / math.sqrt(HEAD_DIM), 1.0)
    w_out_b = w_out.astype(BF16)
    return {
        "attn_width": aw,
        "rope": _rope_lane_tables(seq),
        "norm1_g": norm1_g.reshape(1, -1),
        "w_in": (w_in * col_scale[None, :].astype(F32)).astype(BF16),
        "attn_out_g": attn_out_g.reshape(1, -1),
        "fourier_out_g": fourier_out_g.reshape(1, -1),
        "w_out_a": w_out_b[:aw],
        "w_out_f": w_out_b[aw:],
        "norm2_g": norm2_g.reshape(1, -1),
        "w_up": w_up.astype(BF16),
        "conv_w": conv_w,
        "conv_b": conv_b.reshape(1, -1),
        "w_down": w_down.astype(BF16),
        "final_g": final_g.reshape(1, -1),
    }


def kernel(x_prompt, x_sample, norm1_g, w_in, attn_out_g, fourier_out_g, w_out, norm2_g, w_up,
           conv_w, conv_b, w_down, final_g):
    assert norm1_g.shape[0] == 1, "single-layer encoder"
    assert x_prompt.shape[1] == x_sample.shape[1]
    p = _prepare(x_prompt.shape[1], norm1_g[0], w_in[0], attn_out_g[0], fourier_out_g[0], w_out[0],
                 norm2_g[0], w_up[0], conv_w[0], conv_b[0], w_down[0], final_g)
    return (_encoder(x_prompt, p), _encoder(x_sample, p))
```

```python
import functools
import math

import numpy as np
import jax
import jax.numpy as jnp
from jax import lax
from jax.experimental import pallas as pl
from jax.experimental.pallas import tpu as pltpu

F32 = jnp.float32
BF16 = jnp.bfloat16

HEAD_DIM = 64
ROPE_DIM = 16
ROPE_THETA = 500000.0
N_FOURIER_GROUPS = 4
FOURIER_GROUP = 128
DILATIONS = (1, 4, 16)
BAND = 64
EPS = 1e-6
MASK_VALUE = -1e30
LOG2E = 1.4426950408889634
LN2 = 0.6931471805599453

LANES = 128
VMEM_LIMIT_BYTES = 56 * 1024 * 1024

DFT_N1 = 64
DFT_PAD = 8


def _cparams(semantics):
    return pltpu.CompilerParams(dimension_semantics=semantics,
                                vmem_limit_bytes=VMEM_LIMIT_BYTES)


def _rms(x, g):
    var = jnp.mean(x * x, axis=-1, keepdims=True)
    return x * lax.rsqrt(var + EPS) * g


def _in_proj_kernel(x_ref, g_ref, w_ref, tab_ref, nat_ref, c4_ref, c16_ref,
                    h_ref, acc_ref, s1_ref, s2_ref, *, nj):
    n = pl.program_id(0)
    slot = n % 2
    tm = acc_ref.shape[1]
    n_chunks = acc_ref.shape[2] // LANES

    @pl.when(n == 0)
    def _():
        acc_ref[1] = jnp.zeros(acc_ref.shape[1:], F32)

    @pl.when(n % nj == 0)
    def _():
        h_ref[...] = _rms(x_ref[...], g_ref[...]).astype(BF16)

    cos = tab_ref[0]
    sa = tab_ref[1]
    sb = tab_ref[2]
    for c in range(n_chunks):
        lanes = slice(c * LANES, (c + 1) * LANES)
        a = acc_ref[1 - slot, :, lanes]
        r = a * cos + pltpu.roll(a, LANES - ROPE_DIM // 2, 1) * sa \
            + pltpu.roll(a, ROPE_DIM // 2, 1) * sb
        nat_ref[:, lanes] = r.astype(BF16)
        s1_ref[c] = r
        for r4 in range(4):
            x4 = s1_ref[c, pl.ds(r4, tm // 4, stride=4), :]
            c4_ref[r4, :, lanes] = x4.astype(BF16)
            s2_ref[c, r4] = x4
            for q4 in range(4):
                x16 = s2_ref[c, r4, pl.ds(q4, tm // 16, stride=4), :]
                c16_ref[r4 + 4 * q4, :, lanes] = x16.astype(BF16)

    acc_ref[slot] = jnp.dot(h_ref[...], w_ref[...], preferred_element_type=F32)


def _in_proj(x2, g, w, tabs, *, batch, seq, n_rope_tiles, tm=512):
    ntok, d_model = x2.shape
    nj, _, tn = w.shape
    width = nj * tn
    ntiles = ntok // tm
    assert tm % 256 == 0
    tps = seq // tm
    kern = functools.partial(_in_proj_kernel, nj=nj)

    def prev(n):
        pn = jnp.maximum(n - 1, 0)
        return pn // nj, pn % nj

    def tab_map(n):
        pi, pj = prev(n)
        return (jnp.where(pj < n_rope_tiles, 0, 1), 0, pi % tps, 0)

    def nat_map(n):
        return prev(n)

    def cls_map(n):
        pi, pj = prev(n)
        return (pi // tps, 0, pi % tps, pj)

    return pl.pallas_call(
        kern,
        grid=(ntiles * nj + 1,),
        in_specs=[
            pl.BlockSpec((tm, d_model), lambda n: (jnp.minimum(n // nj, ntiles - 1), 0)),
            pl.BlockSpec((1, d_model), lambda n: (0, 0)),
            pl.BlockSpec((None, d_model, tn), lambda n: (n % nj, 0, 0)),
            pl.BlockSpec((None, 3, tm, LANES), tab_map),
        ],
        out_specs=[
            pl.BlockSpec((tm, tn), nat_map),
            pl.BlockSpec((None, 4, tm // 4, tn), cls_map),
            pl.BlockSpec((None, 16, tm // 16, tn), cls_map),
        ],
        out_shape=[
            jax.ShapeDtypeStruct((ntok, width), BF16),
            jax.ShapeDtypeStruct((batch, 4, seq // 4, width), BF16),
            jax.ShapeDtypeStruct((batch, 16, seq // 16, width), BF16),
        ],
        scratch_shapes=[pltpu.VMEM((tm, d_model), BF16),
                        pltpu.VMEM((2, tm, tn), F32),
                        pltpu.VMEM((tn // LANES, tm, LANES), F32),
                        pltpu.VMEM((tn // LANES, 4, tm // 4, LANES), F32)],
        compiler_params=_cparams(("arbitrary",)),
        name="in_proj",
    )(x2, g, w, tabs)


def _attn_kernel(bias_ref, q_ref, kp_ref, kc_ref, kn_ref, vp_ref, vc_ref, vn_ref,
                 o_ref, lse_ref, v0_ref, v1_ref, s_ref, p_ref, m_ref,
                 *, tb, tq, n_chunks, pairs, row_chunk=32):
    j = pl.program_id(2)
    tk = tq + 2 * BAND
    n_sub = tb // tq

    lane = lax.broadcasted_iota(jnp.int32, (tq, LANES), 1)
    first_head = lane < HEAD_DIM
    first_chunk = lax.broadcasted_iota(jnp.int32, (row_chunk, LANES), 1) < HEAD_DIM
    first_w = lax.broadcasted_iota(jnp.int32, (tk, LANES), 1) < HEAD_DIM
    ones = jnp.ones((tk, LANES), BF16)

    def window(prev_ref, cur_ref, next_ref, i, lanes):
        if i == 0:
            return jnp.concatenate([prev_ref[:, lanes], cur_ref[0:tq + BAND, lanes]], axis=0)
        if i == n_sub - 1:
            return jnp.concatenate([cur_ref[tb - tq - BAND:tb, lanes], next_ref[:, lanes]], axis=0)
        return cur_ref[i * tq - BAND:(i + 1) * tq + BAND, lanes]

    units = [(i, p) for i in range(n_sub) for p in range(pairs)]

    def stage_a(u):
        i, p = units[u]
        slot = u % 2
        lanes = slice(p * LANES, (p + 1) * LANES)
        vwin = window(vp_ref, vc_ref, vn_ref, i, lanes)
        v0_ref[u % 3] = jnp.where(first_w, vwin, ones)
        v1_ref[u % 3] = jnp.where(first_w, ones, vwin)
        q = q_ref[i * tq:(i + 1) * tq, lanes]
        zero = jnp.zeros_like(q)
        q2 = jnp.concatenate([jnp.where(first_head, q, zero), jnp.where(first_head, zero, q)],
                             axis=0)
        kw = window(kp_ref, kc_ref, kn_ref, i, lanes)
        s = lax.dot_general(q2, kw, (((1,), (1,)), ((), ())), preferred_element_type=F32)
        if i == 0:
            idx = jnp.where(j == 0, 1, 0)
        elif i == n_sub - 1:
            idx = jnp.where(j == n_chunks - 1, 2, 0)
        else:
            idx = 0
        s_ref[slot] = s + bias_ref[idx]

    def stage_b(u):
        slot = u % 2
        for r in range(0, tq, row_chunk):
            s0 = s_ref[slot, r:r + row_chunk, :]
            s1 = s_ref[slot, tq + r:tq + r + row_chunk, :]
            m0 = jnp.max(s0, axis=-1, keepdims=True)
            m1 = jnp.max(s1, axis=-1, keepdims=True)
            p_ref[slot, r:r + row_chunk, :] = jnp.exp2(s0 - m0).astype(BF16)
            p_ref[slot, tq + r:tq + r + row_chunk, :] = jnp.exp2(s1 - m1).astype(BF16)
            m_ref[slot, r:r + row_chunk, :] = jnp.where(
                first_chunk, jnp.broadcast_to(m0, (row_chunk, LANES)),
                jnp.broadcast_to(m1, (row_chunk, LANES)))

    def stage_c(u):
        i, p = units[u]
        slot = u % 2
        lanes = slice(p * LANES, (p + 1) * LANES)
        pv0 = jnp.dot(p_ref[slot, 0:tq, :], v0_ref[u % 3], preferred_element_type=F32)
        pv1 = jnp.dot(p_ref[slot, tq:2 * tq, :], v1_ref[u % 3], preferred_element_type=F32)
        o = jnp.where(first_head, pv0, pv1)
        dens = pltpu.roll(jnp.where(first_head, pv1, pv0), HEAD_DIM, 1)
        o_ref[i * tq:(i + 1) * tq, lanes] = (o * (1.0 / dens)).astype(BF16)
        lse = m_ref[slot] * LN2 + jnp.log(dens)
        cur = jnp.zeros((tq, LANES), F32) if p == 0 else lse_ref[i * tq:(i + 1) * tq, :]
        lse_ref[i * tq:(i + 1) * tq, :] = jnp.where(
            jnp.bitwise_and(lane, HEAD_DIM - 1) == p, lse, cur)

    n_units = len(units)
    for t in range(n_units + 2):
        if t < n_units:
            stage_a(t)
        if 0 <= t - 1 < n_units:
            stage_b(t - 1)
        if 0 <= t - 2 < n_units:
            stage_c(t - 2)


def _attn_bias(tq):
    tk = tq + 2 * BAND
    col = np.arange(tk)[None, :]
    row = np.arange(tq)[:, None]
    band = (col - row >= 0) & (col - row <= 2 * BAND)
    variants = [band, band & (col >= BAND), band & (col < tq + BAND)]
    bias = np.stack([np.where(v, 0.0, MASK_VALUE) for v in variants]).astype(np.float32)
    return jnp.asarray(np.concatenate([bias, bias], axis=1))


def _attention(qkv, *, attn_width, tq=128, max_tb=512):
    batch, d, t_len, width = qkv.shape
    tb = min(max_tb, t_len)
    assert t_len % tb == 0 and tb % tq == 0 and tb // tq >= 2
    n_chunks = t_len // tb
    pairs = attn_width // LANES
    hb = tb // BAND
    last_hb = t_len // BAND - 1
    tk = tq + 2 * BAND

    def cur(off):
        return pl.BlockSpec((None, None, tb, attn_width), lambda b, r, j: (b, r, j, off))

    def prv(off):
        return pl.BlockSpec((None, None, BAND, attn_width),
                            lambda b, r, j: (b, r, jnp.maximum(j * hb - 1, 0), off))

    def nxt(off):
        return pl.BlockSpec((None, None, BAND, attn_width),
                            lambda b, r, j: (b, r, jnp.minimum((j + 1) * hb, last_hb), off))

    kern = functools.partial(_attn_kernel, tb=tb, tq=tq, n_chunks=n_chunks, pairs=pairs)
    return pl.pallas_call(
        kern,
        grid=(batch, d, n_chunks),
        in_specs=[pl.BlockSpec((3, 2 * tq, tk), lambda b, r, j: (0, 0, 0)),
                  cur(0), prv(1), cur(1), nxt(1), prv(2), cur(2), nxt(2)],
        out_specs=[
            pl.BlockSpec((None, None, tb, attn_width), lambda b, r, j: (b, r, j, 0)),
            pl.BlockSpec((None, None, tb, LANES), lambda b, r, j: (b, r, j, 0)),
        ],
        out_shape=[
            jax.ShapeDtypeStruct((batch, d, t_len, attn_width), BF16),
            jax.ShapeDtypeStruct((batch, d, t_len, LANES), F32),
        ],
        scratch_shapes=[pltpu.VMEM((3, tk, LANES), BF16),
                        pltpu.VMEM((3, tk, LANES), BF16),
                        pltpu.VMEM((2, 2 * tq, tk), F32),
                        pltpu.VMEM((2, 2 * tq, tk), BF16),
                        pltpu.VMEM((2, tq, LANES), F32)],
        compiler_params=_cparams(("parallel", "parallel", "parallel")),
        name=f"attn_d{d}",
    )(_attn_bias(tq), qkv, qkv, qkv, qkv, qkv, qkv, qkv)


def _fourier_kernel(x_ref, m1_ref, m2_ref, csc_ref, o_ref, xf_ref, yr_ref, yi_ref,
                    *, n1, n2, pitch, scale):
    xf_ref[...] = x_ref[...].astype(F32)

    def stage1(m, carry):
        xm = xf_ref[pl.ds(m, n1, stride=n2), :].astype(BF16)
        y = jnp.dot(m1_ref[m], xm, preferred_element_type=F32)
        yr_ref[pl.ds(m, n1, stride=pitch), :] = y[:n1]
        yi_ref[pl.ds(m, n1, stride=pitch), :] = y[n1:]
        return carry

    lax.fori_loop(0, n2, stage1, 0, unroll=8)

    m2 = m2_ref[...]
    csc = csc_ref[...]

    def stage2(kk, carry):
        ka = 2 * kk
        kb = ka + 1
        base_a = pl.multiple_of(ka * pitch, 8)
        base_b = pl.multiple_of(kb * pitch, 8)
        dat_a = jnp.concatenate([yr_ref[pl.ds(base_a, n2), :], yi_ref[pl.ds(base_a, n2), :]], axis=0)
        dat_b = jnp.concatenate([yr_ref[pl.ds(base_b, n2), :], yi_ref[pl.ds(base_b, n2), :]], axis=0)
        dat = jnp.concatenate([dat_a, dat_b], axis=1).astype(BF16)
        o = jnp.dot(m2, dat, preferred_element_type=F32).astype(BF16)
        lhs = jnp.concatenate(
            [jnp.concatenate([o[:n2, :LANES], o[n2:, :LANES]], axis=1),
             jnp.concatenate([o[:n2, LANES:], o[n2:, LANES:]], axis=1)], axis=0)
        res = jnp.dot(lhs, csc, preferred_element_type=F32) * scale
        o_ref[pl.ds(ka, n2, stride=n1), :] = res[:n2]
        o_ref[pl.ds(kb, n2, stride=n1), :] = res[n2:]
        return carry

    lax.fori_loop(0, n1 // 2, stage2, 0, unroll=4)


def _dft_mats(n):
    k = np.arange(n)
    ang = 2.0 * np.pi * ((k[:, None] * k[None, :]) % n) / n
    return np.cos(ang), np.sin(ang)


def _fourier(f3, *, first_lane_block=0):
    batch, seq, _ = f3.shape
    ch = FOURIER_GROUP
    groups = N_FOURIER_GROUPS
    width = groups * ch
    n1 = DFT_N1
    n2 = seq // n1
    pitch = n2 + DFT_PAD
    pos = n2 * np.arange(n1)[None, None, :] + np.arange(n2)[:, None, None]
    ang = 2.0 * np.pi * ((np.arange(n1)[None, :, None] * pos) % seq) / seq
    m1 = jnp.asarray(np.concatenate([np.cos(ang), -np.sin(ang)], axis=1), F32).astype(BF16)
    c2, s2 = _dft_mats(n2)
    m2 = jnp.asarray(np.block([[c2, s2], [-s2, c2]]), F32).astype(BF16)
    cch, sch = _dft_mats(ch)
    csc = jnp.asarray(np.concatenate([cch, sch], axis=0), F32).astype(BF16)
    assert n1 % 2 == 0
    kern = functools.partial(_fourier_kernel, n1=n1, n2=n2, pitch=pitch,
                             scale=1.0 / math.sqrt(seq * ch))
    return pl.pallas_call(
        kern,
        grid=(batch, groups),
        in_specs=[
            pl.BlockSpec((None, seq, ch), lambda b, g: (b, 0, first_lane_block + g)),
            pl.BlockSpec((n2, 2 * n1, n1), lambda b, g: (0, 0, 0)),
            pl.BlockSpec((2 * n2, 2 * n2), lambda b, g: (0, 0)),
            pl.BlockSpec((2 * ch, ch), lambda b, g: (0, 0)),
        ],
        out_specs=pl.BlockSpec((None, seq, ch), lambda b, g: (b, 0, g)),
        out_shape=jax.ShapeDtypeStruct((batch, seq, width), F32),
        scratch_shapes=[pltpu.VMEM((seq, ch), F32),
                        pltpu.VMEM((n1 * pitch, ch), F32),
                        pltpu.VMEM((n1 * pitch, ch), F32)],
        compiler_params=_cparams(("parallel", "parallel")),
        name="fourier",
    )(f3, m1, m2, csc)


def _out_proj_kernel(o1_ref, o4_ref, o16_ref, l1_ref, l4_ref, l16_ref, four_ref, x_ref,
                     ga_ref, gf_ref, wa_ref, wf_ref, g2_ref, e_ref, x1_ref, h2_ref,
                     n4_ref, n16_ref, ln4_ref, ln16_ref):
    tm = x_ref.shape[0]
    n_slabs = n4_ref.shape[0]

    for r in range(4):
        blk = o4_ref[r].astype(F32)
        for c in range(n_slabs):
            n4_ref[c, pl.ds(r, tm // 4, stride=4), :] = blk[:, c * LANES:(c + 1) * LANES]
        ln4_ref[pl.ds(r, tm // 4, stride=4), :] = l4_ref[r]
    for r in range(16):
        blk = o16_ref[r].astype(F32)
        for c in range(n_slabs):
            n16_ref[c, pl.ds(r, tm // 16, stride=16), :] = blk[:, c * LANES:(c + 1) * LANES]
        ln16_ref[pl.ds(r, tm // 16, stride=16), :] = l16_ref[r]

    l1 = l1_ref[...]
    l2 = ln4_ref[...]
    l3 = ln16_ref[...]
    m = jnp.maximum(jnp.maximum(l1, l2), l3)
    e1 = jnp.exp(l1 - m)
    e2 = jnp.exp(l2 - m)
    e3 = jnp.exp(l3 - m)
    inv = 1.0 / (e1 + e2 + e3)
    expand = e_ref[...]

    def spread(w):
        hi = w.astype(BF16)
        lo = (w - hi.astype(F32)).astype(BF16)
        return jnp.dot(hi, expand, preferred_element_type=F32) \
            + jnp.dot(lo, expand, preferred_element_type=F32)

    w1 = spread(e1 * inv)
    w2 = spread(e2 * inv)
    o1 = o1_ref[...].astype(F32)
    slabs = []
    ssq = jnp.zeros((tm, 1), F32)
    for c in range(n_slabs):
        lanes = slice(c * LANES, (c + 1) * LANES)
        o3 = n16_ref[c]
        a = o3 + w1[:, lanes] * (o1[:, lanes] - o3) + w2[:, lanes] * (n4_ref[c] - o3)
        ssq = ssq + jnp.sum(a * a, axis=-1, keepdims=True)
        slabs.append(a)
    attn = jnp.concatenate(slabs, axis=1)
    width = n_slabs * LANES
    a_n = (attn * lax.rsqrt(ssq / width + EPS) * ga_ref[...]).astype(BF16)
    f_n = _rms(four_ref[...], gf_ref[...]).astype(BF16)
    y = jnp.dot(a_n, wa_ref[...], preferred_element_type=F32) \
        + jnp.dot(f_n, wf_ref[...], preferred_element_type=F32)
    x1 = x_ref[...] + y
    x1_ref[...] = x1
    h2_ref[...] = _rms(x1, g2_ref[...]).astype(BF16)


def _out_proj(o1, o4, o16, l1, l4, l16, four, x2, ga, gf, wa, wf, g2, *, seq, tm=256):
    ntok, d_model = x2.shape
    aw = o1.shape[1]
    fw = four.shape[1]
    n_heads = aw // HEAD_DIM
    tps = seq // tm
    expand = np.zeros((LANES, aw), np.float32)
    for h in range(n_heads):
        expand[(h % 2) * HEAD_DIM + h // 2, h * HEAD_DIM:(h + 1) * HEAD_DIM] = 1.0
    expand = jnp.asarray(expand).astype(BF16)
    row = lambda w: pl.BlockSpec((tm, w), lambda i: (i, 0))
    cls = lambda d, w: pl.BlockSpec((None, d, tm // d, w), lambda i: (i // tps, 0, i % tps, 0))
    const = lambda a, b: pl.BlockSpec((a, b), lambda i: (0, 0))
    return pl.pallas_call(
        _out_proj_kernel,
        grid=(ntok // tm,),
        in_specs=[row(aw), cls(4, aw), cls(16, aw), row(LANES), cls(4, LANES), cls(16, LANES),
                  row(fw), row(d_model),
                  const(1, aw), const(1, fw), const(aw, d_model), const(fw, d_model),
                  const(1, d_model), const(LANES, aw)],
        out_specs=[row(d_model), row(d_model)],
        out_shape=[jax.ShapeDtypeStruct((ntok, d_model), F32),
                   jax.ShapeDtypeStruct((ntok, d_model), BF16)],
        scratch_shapes=[pltpu.VMEM((aw // LANES, tm, LANES), F32),
                        pltpu.VMEM((aw // LANES, tm, LANES), F32),
                        pltpu.VMEM((tm, LANES), F32),
                        pltpu.VMEM((tm, LANES), F32)],
        compiler_params=_cparams(("parallel",)),
        name="out_proj",
    )(o1, o4, o16, l1, l4, l16, four, x2, ga, gf, wa, wf, g2, expand)


HALO = 16


def _up_kernel(hp_ref, hc_ref, hn_ref, wg_ref, wv_ref, cwg_ref, cwv_ref, cbg_ref, cbv_ref,
               o_ref, lhs_ref, ug_ref, uv_ref, *, tm, tiles_per_seq):
    i = pl.program_id(0)
    j = pl.program_id(1)

    @pl.when(j == 0)
    def _():
        t = i % tiles_per_seq
        zero = jnp.zeros((HALO, lhs_ref.shape[1]), BF16)
        lhs_ref[0:HALO] = jnp.where(t == 0, zero, hp_ref[...])
        lhs_ref[HALO:HALO + tm] = hc_ref[...]
        lhs_ref[HALO + tm:HALO + tm + HALO] = jnp.where(t == tiles_per_seq - 1, zero, hn_ref[...])

    lhs = lhs_ref[...]
    ug_ref[...] = jnp.dot(lhs, wg_ref[...], preferred_element_type=F32)
    uv_ref[...] = jnp.dot(lhs, wv_ref[...], preferred_element_type=F32)

    def conv(u_ref, cw_ref, cb_ref):
        cw = cw_ref[...]
        return (u_ref[HALO - 1:HALO - 1 + tm] * cw[0:1]
                + u_ref[HALO:HALO + tm] * cw[1:2]
                + u_ref[HALO + 1:HALO + 1 + tm] * cw[2:3]
                + cb_ref[...])

    gate = conv(ug_ref, cwg_ref, cbg_ref)
    val = conv(uv_ref, cwv_ref, cbv_ref)
    o_ref[...] = (gate * (1.0 / (1.0 + jnp.exp(-gate))) * val).astype(BF16)


def _up_proj(h2, w_up, conv_w, conv_b, *, seq, tm=512):
    ntok, d_model = h2.shape
    nj = w_up.shape[0] // 2
    tn = w_up.shape[2]
    d_ff = nj * tn
    tps = seq // tm
    hb = tm // HALO
    last = ntok // HALO - 1
    kern = functools.partial(_up_kernel, tm=tm, tiles_per_seq=tps)
    return pl.pallas_call(
        kern,
        grid=(ntok // tm, nj),
        in_specs=[
            pl.BlockSpec((HALO, d_model), lambda i, j: (jnp.maximum(i * hb - 1, 0), 0)),
            pl.BlockSpec((tm, d_model), lambda i, j: (i, 0)),
            pl.BlockSpec((HALO, d_model), lambda i, j: (jnp.minimum((i + 1) * hb, last), 0)),
            pl.BlockSpec((None, d_model, tn), lambda i, j: (j, 0, 0)),
            pl.BlockSpec((None, d_model, tn), lambda i, j: (nj + j, 0, 0)),
            pl.BlockSpec((3, tn), lambda i, j: (0, j)),
            pl.BlockSpec((3, tn), lambda i, j: (0, nj + j)),
            pl.BlockSpec((1, tn), lambda i, j: (0, j)),
            pl.BlockSpec((1, tn), lambda i, j: (0, nj + j)),
        ],
        out_specs=pl.BlockSpec((tm, tn), lambda i, j: (i, j)),
        out_shape=jax.ShapeDtypeStruct((ntok, d_ff), BF16),
        scratch_shapes=[pltpu.VMEM((tm + 2 * HALO, d_model), BF16),
                        pltpu.VMEM((tm + 2 * HALO, tn), F32),
                        pltpu.VMEM((tm + 2 * HALO, tn), F32)],
        compiler_params=_cparams(("parallel", "arbitrary")),
        name="up_proj",
    )(h2, h2, h2, w_up, w_up, conv_w, conv_w, conv_b, conv_b)


def _down_kernel(a_ref, w_ref, x1_ref, g_ref, y_ref, *, tn, n_blocks):
    n = pl.program_id(1)
    y = jnp.dot(a_ref[...], w_ref[...], preferred_element_type=F32)
    for b in range(n_blocks):
        @pl.when(n == b)
        def _(b=b):
            cols = slice(b * tn, (b + 1) * tn)
            y_ref[:, cols] = x1_ref[:, cols] + y

    @pl.when(n == n_blocks - 1)
    def _():
        y_ref[...] = _rms(y_ref[...], g_ref[...])


def _down_proj(act, w_down, x1, final_g, *, tm=512):
    ntok, d_ff = act.shape
    n_blocks, _, tn = w_down.shape
    d_model = n_blocks * tn
    kern = functools.partial(_down_kernel, tn=tn, n_blocks=n_blocks)
    return pl.pallas_call(
        kern,
        grid=(ntok // tm, n_blocks),
        in_specs=[
            pl.BlockSpec((tm, d_ff), lambda i, n: (i, 0)),
            pl.BlockSpec((None, d_ff, tn), lambda i, n: (n, 0, 0)),
            pl.BlockSpec((tm, d_model), lambda i, n: (i, 0)),
            pl.BlockSpec((1, d_model), lambda i, n: (0, 0)),
        ],
        out_specs=pl.BlockSpec((tm, d_model), lambda i, n: (i, 0)),
        out_shape=jax.ShapeDtypeStruct((ntok, d_model), F32),
        compiler_params=_cparams(("parallel", "arbitrary")),
        name="down_proj",
    )(act, w_down, x1, final_g)


def _rope_lane_tables(seq):
    inv_freq = ROPE_THETA ** (-jnp.arange(0, ROPE_DIM, 2, dtype=F32) / ROPE_DIM)
    ang = jnp.arange(seq, dtype=F32)[:, None] * inv_freq[None, :]
    cos, sin = jnp.cos(ang), jnp.sin(ang)
    half = ROPE_DIM // 2
    rest = HEAD_DIM - ROPE_DIM
    cos_h = jnp.concatenate([cos, cos, jnp.ones((seq, rest), F32)], axis=1)
    sa_h = jnp.concatenate([-sin, jnp.zeros((seq, HEAD_DIM - half), F32)], axis=1)
    sb_h = jnp.concatenate([jnp.zeros((seq, half), F32), sin, jnp.zeros((seq, rest), F32)], axis=1)
    reps = LANES // HEAD_DIM
    rope = jnp.stack([jnp.tile(t, (1, reps)) for t in (cos_h, sa_h, sb_h)])
    ident = jnp.stack([jnp.ones((seq, LANES), F32), jnp.zeros((seq, LANES), F32),
                       jnp.zeros((seq, LANES), F32)])
    return jnp.stack([rope, ident])


def _encoder(x, p):
    batch, seq, d_model = x.shape
    aw = p["attn_width"]
    ntok = batch * seq
    x2 = x.reshape(ntok, d_model)

    nat, c4, c16 = _in_proj(x2, p["norm1_g"], p["w_in"], p["rope"], batch=batch, seq=seq,
                            n_rope_tiles=2 * aw // WEIGHT_TN)
    nat3 = nat.reshape(batch, seq, -1)
    o1, l1 = _attention(nat3.reshape(batch, 1, seq, -1), attn_width=aw)
    o4, l4 = _attention(c4, attn_width=aw)
    o16, l16 = _attention(c16, attn_width=aw)
    four = _fourier(nat3, first_lane_block=3 * aw // FOURIER_GROUP)

    x1, h2 = _out_proj(o1.reshape(ntok, aw), o4, o16, l1.reshape(ntok, LANES), l4, l16,
                       four.reshape(ntok, -1), x2,
                       p["attn_out_g"], p["fourier_out_g"], p["w_out_a"], p["w_out_f"],
                       p["norm2_g"], seq=seq)
    act = _up_proj(h2, p["w_up"], p["conv_w"], p["conv_b"], seq=seq)
    y = _down_proj(act, p["w_down"], x1, p["final_g"])
    return y.reshape(batch, seq, d_model)


WEIGHT_TN = 512


def _column_blocks(w):
    k, n = w.shape
    return w.astype(BF16).reshape(k, n // WEIGHT_TN, WEIGHT_TN).transpose(1, 0, 2)


def _prepare(seq, norm1_g, w_in, attn_out_g, fourier_out_g, w_out, norm2_g, w_up, conv_w, conv_b,
             w_down, final_g):
    aw = (w_in.shape[1] - N_FOURIER_GROUPS * FOURIER_GROUP) // 3
    assert (2 * aw) % WEIGHT_TN == 0 and (3 * aw) % WEIGHT_TN == 0
    col_scale = jnp.where(jnp.arange(w_in.shape[1]) < aw, LOG2E / math.sqrt(HEAD_DIM), 1.0)
    w_out_b = w_out.astype(BF16)
    return {
        "attn_width": aw,
        "rope": _rope_lane_tables(seq),
        "norm1_g": norm1_g.reshape(1, -1),
        "w_in": _column_blocks(w_in * col_scale[None, :].astype(F32)),
        "attn_out_g": attn_out_g.reshape(1, -1),
        "fourier_out_g": fourier_out_g.reshape(1, -1),
        "w_out_a": w_out_b[:aw],
        "w_out_f": w_out_b[aw:],
        "norm2_g": norm2_g.reshape(1, -1),
        "w_up": _column_blocks(w_up),
        "conv_w": conv_w,
        "conv_b": conv_b.reshape(1, -1),
        "w_down": _column_blocks(w_down),
        "final_g": final_g.reshape(1, -1),
    }


def kernel(x_prompt, x_sample, norm1_g, w_in, attn_out_g, fourier_out_g, w_out, norm2_g, w_up,
           conv_w, conv_b, w_down, final_g):
    assert norm1_g.shape[0] == 1, "single-layer encoder"
    assert x_prompt.shape[1] == x_sample.shape[1]
    p = _prepare(x_prompt.shape[1], norm1_g[0], w_in[0], attn_out_g[0], fourier_out_g[0], w_out[0],
                 norm2_g[0], w_up[0], conv_w[0], conv_b[0], w_down[0], final_g)
    return (_encoder(x_prompt, p), _encoder(x_sample, p))
```

```python
import functools
import math

import numpy as np
import jax
import jax.numpy as jnp
from jax import lax
from jax.experimental import pallas as pl
from jax.experimental.pallas import tpu as pltpu

F32 = jnp.float32
BF16 = jnp.bfloat16

HEAD_DIM = 64
ROPE_DIM = 16
ROPE_THETA = 500000.0
N_FOURIER_GROUPS = 4
FOURIER_GROUP = 128
DILATIONS = (1, 4, 16)
BAND = 64
EPS = 1e-6
MASK_VALUE = -1e30
LOG2E = 1.4426950408889634
LN2 = 0.6931471805599453

LANES = 128
VMEM_LIMIT_BYTES = 56 * 1024 * 1024

DFT_N1 = 64
DFT_PAD = 8


def _cparams(semantics):
    return pltpu.CompilerParams(dimension_semantics=semantics,
                                vmem_limit_bytes=VMEM_LIMIT_BYTES)


def _rms(x, g):
    var = jnp.mean(x * x, axis=-1, keepdims=True)
    return x * lax.rsqrt(var + EPS) * g


def _in_proj_kernel(x_ref, g_ref, w_ref, tab_ref, nat_ref, c4_ref, c16_ref,
                    h_ref, acc_ref, s1_ref, s2_ref, *, nj):
    n = pl.program_id(0)
    slot = n % 2
    tm = acc_ref.shape[1]
    n_chunks = acc_ref.shape[2] // LANES

    @pl.when(n == 0)
    def _():
        acc_ref[1] = jnp.zeros(acc_ref.shape[1:], F32)

    @pl.when(n % nj == 0)
    def _():
        h_ref[...] = _rms(x_ref[...], g_ref[...]).astype(BF16)

    cos = tab_ref[0]
    sa = tab_ref[1]
    sb = tab_ref[2]
    for c in range(n_chunks):
        lanes = slice(c * LANES, (c + 1) * LANES)
        a = acc_ref[1 - slot, :, lanes]
        r = a * cos + pltpu.roll(a, LANES - ROPE_DIM // 2, 1) * sa \
            + pltpu.roll(a, ROPE_DIM // 2, 1) * sb
        nat_ref[:, lanes] = r.astype(BF16)
        s1_ref[c] = r
        for r4 in range(4):
            x4 = s1_ref[c, pl.ds(r4, tm // 4, stride=4), :]
            c4_ref[r4, :, lanes] = x4.astype(BF16)
            s2_ref[c, r4] = x4
            for q4 in range(4):
                x16 = s2_ref[c, r4, pl.ds(q4, tm // 16, stride=4), :]
                c16_ref[r4 + 4 * q4, :, lanes] = x16.astype(BF16)

    acc_ref[slot] = jnp.dot(h_ref[...], w_ref[...], preferred_element_type=F32)


def _in_proj(x2, g, w, tabs, *, batch, seq, n_rope_tiles, tm=1024, tn=512):
    ntok, d_model = x2.shape
    width = w.shape[1]
    nj = width // tn
    ntiles = ntok // tm
    assert tm % 256 == 0
    tps = seq // tm
    kern = functools.partial(_in_proj_kernel, nj=nj)

    def prev(n):
        pn = jnp.maximum(n - 1, 0)
        return pn // nj, pn % nj

    def tab_map(n):
        pi, pj = prev(n)
        return (jnp.where(pj < n_rope_tiles, 0, 1), 0, pi % tps, 0)

    def nat_map(n):
        return prev(n)

    def cls_map(n):
        pi, pj = prev(n)
        return (pi // tps, 0, pi % tps, pj)

    return pl.pallas_call(
        kern,
        grid=(ntiles * nj + 1,),
        in_specs=[
            pl.BlockSpec((tm, d_model), lambda n: (jnp.minimum(n // nj, ntiles - 1), 0)),
            pl.BlockSpec((1, d_model), lambda n: (0, 0)),
            pl.BlockSpec((d_model, tn), lambda n: (0, n % nj)),
            pl.BlockSpec((None, 3, tm, LANES), tab_map),
        ],
        out_specs=[
            pl.BlockSpec((tm, tn), nat_map),
            pl.BlockSpec((None, 4, tm // 4, tn), cls_map),
            pl.BlockSpec((None, 16, tm // 16, tn), cls_map),
        ],
        out_shape=[
            jax.ShapeDtypeStruct((ntok, width), BF16),
            jax.ShapeDtypeStruct((batch, 4, seq // 4, width), BF16),
            jax.ShapeDtypeStruct((batch, 16, seq // 16, width), BF16),
        ],
        scratch_shapes=[pltpu.VMEM((tm, d_model), BF16),
                        pltpu.VMEM((2, tm, tn), F32),
                        pltpu.VMEM((tn // LANES, tm, LANES), F32),
                        pltpu.VMEM((tn // LANES, 4, tm // 4, LANES), F32)],
        compiler_params=_cparams(("arbitrary",)),
        name="in_proj",
    )(x2, g, w, tabs)


def _attn_kernel(bias_ref, q_ref, kp_ref, kc_ref, kn_ref, vp_ref, vc_ref, vn_ref,
                 o_ref, lse_ref, s_ref, p_ref, m_ref, d_ref,
                 *, tb, tq, n_chunks, pairs, row_chunk=32):
    j = pl.program_id(2)
    tk = tq + 2 * BAND
    n_sub = tb // tq

    lane = lax.broadcasted_iota(jnp.int32, (tq, LANES), 1)
    first_head = lane < HEAD_DIM
    first_chunk = lax.broadcasted_iota(jnp.int32, (row_chunk, LANES), 1) < HEAD_DIM

    def window(prev_ref, cur_ref, next_ref, i, lanes):
        if i == 0:
            return jnp.concatenate([prev_ref[:, lanes], cur_ref[0:tq + BAND, lanes]], axis=0)
        if i == n_sub - 1:
            return jnp.concatenate([cur_ref[tb - tq - BAND:tb, lanes], next_ref[:, lanes]], axis=0)
        return cur_ref[i * tq - BAND:(i + 1) * tq + BAND, lanes]

    units = [(i, p) for i in range(n_sub) for p in range(pairs)]

    def stage_a(u):
        i, p = units[u]
        slot = u % 2
        lanes = slice(p * LANES, (p + 1) * LANES)
        q = q_ref[i * tq:(i + 1) * tq, lanes]
        zero = jnp.zeros_like(q)
        q2 = jnp.concatenate([jnp.where(first_head, q, zero), jnp.where(first_head, zero, q)],
                             axis=0)
        kw = window(kp_ref, kc_ref, kn_ref, i, lanes)
        s = lax.dot_general(q2, kw, (((1,), (1,)), ((), ())), preferred_element_type=F32)
        if i == 0:
            idx = jnp.where(j == 0, 1, 0)
        elif i == n_sub - 1:
            idx = jnp.where(j == n_chunks - 1, 2, 0)
        else:
            idx = 0
        s_ref[slot] = s + bias_ref[idx]

    def stage_b(u):
        slot = u % 2
        for r in range(0, tq, row_chunk):
            s0 = s_ref[slot, r:r + row_chunk, :]
            s1 = s_ref[slot, tq + r:tq + r + row_chunk, :]
            m0 = jnp.max(s0, axis=-1, keepdims=True)
            m1 = jnp.max(s1, axis=-1, keepdims=True)
            e0 = jnp.exp2(s0 - m0)
            e1 = jnp.exp2(s1 - m1)
            p_ref[slot, r:r + row_chunk, :] = e0.astype(BF16)
            p_ref[slot, tq + r:tq + r + row_chunk, :] = e1.astype(BF16)
            m_ref[slot, r:r + row_chunk, :] = jnp.where(
                first_chunk, jnp.broadcast_to(m0, (row_chunk, LANES)),
                jnp.broadcast_to(m1, (row_chunk, LANES)))
            d0 = jnp.sum(e0, axis=-1, keepdims=True)
            d1 = jnp.sum(e1, axis=-1, keepdims=True)
            d_ref[slot, r:r + row_chunk, :] = jnp.where(
                first_chunk, jnp.broadcast_to(d0, (row_chunk, LANES)),
                jnp.broadcast_to(d1, (row_chunk, LANES)))

    def stage_c(u):
        i, p = units[u]
        slot = u % 2
        lanes = slice(p * LANES, (p + 1) * LANES)
        pv = jnp.dot(p_ref[slot], window(vp_ref, vc_ref, vn_ref, i, lanes),
                     preferred_element_type=F32)
        o = jnp.where(first_head, pv[:tq], pv[tq:])
        dens = d_ref[slot]
        o_ref[i * tq:(i + 1) * tq, lanes] = (o * (1.0 / dens)).astype(BF16)
        lse = m_ref[slot] * LN2 + jnp.log(dens)
        cur = jnp.zeros((tq, LANES), F32) if p == 0 else lse_ref[i * tq:(i + 1) * tq, :]
        lse_ref[i * tq:(i + 1) * tq, :] = jnp.where(
            jnp.bitwise_and(lane, HEAD_DIM - 1) == p, lse, cur)

    n_units = len(units)
    for t in range(n_units + 2):
        if t < n_units:
            stage_a(t)
        if 0 <= t - 1 < n_units:
            stage_b(t - 1)
        if 0 <= t - 2 < n_units:
            stage_c(t - 2)


def _attn_bias(tq):
    tk = tq + 2 * BAND
    col = np.arange(tk)[None, :]
    row = np.arange(tq)[:, None]
    band = (col - row >= 0) & (col - row <= 2 * BAND)
    variants = [band, band & (col >= BAND), band & (col < tq + BAND)]
    bias = np.stack([np.where(v, 0.0, MASK_VALUE) for v in variants]).astype(np.float32)
    return jnp.asarray(np.concatenate([bias, bias], axis=1))


def _attention(qkv, *, attn_width, tq=128, max_tb=512):
    batch, d, t_len, width = qkv.shape
    tb = min(max_tb, t_len)
    assert t_len % tb == 0 and tb % tq == 0 and tb // tq >= 2
    n_chunks = t_len // tb
    pairs = attn_width // LANES
    hb = tb // BAND
    last_hb = t_len // BAND - 1
    tk = tq + 2 * BAND

    def cur(off):
        return pl.BlockSpec((None, None, tb, attn_width), lambda b, r, j: (b, r, j, off))

    def prv(off):
        return pl.BlockSpec((None, None, BAND, attn_width),
                            lambda b, r, j: (b, r, jnp.maximum(j * hb - 1, 0), off))

    def nxt(off):
        return pl.BlockSpec((None, None, BAND, attn_width),
                            lambda b, r, j: (b, r, jnp.minimum((j + 1) * hb, last_hb), off))

    kern = functools.partial(_attn_kernel, tb=tb, tq=tq, n_chunks=n_chunks, pairs=pairs)
    return pl.pallas_call(
        kern,
        grid=(batch, d, n_chunks),
        in_specs=[pl.BlockSpec((3, 2 * tq, tk), lambda b, r, j: (0, 0, 0)),
                  cur(0), prv(1), cur(1), nxt(1), prv(2), cur(2), nxt(2)],
        out_specs=[
            pl.BlockSpec((None, None, tb, attn_width), lambda b, r, j: (b, r, j, 0)),
            pl.BlockSpec((None, None, tb, LANES), lambda b, r, j: (b, r, j, 0)),
        ],
        out_shape=[
            jax.ShapeDtypeStruct((batch, d, t_len, attn_width), BF16),
            jax.ShapeDtypeStruct((batch, d, t_len, LANES), F32),
        ],
        scratch_shapes=[pltpu.VMEM((2, 2 * tq, tk), F32),
                        pltpu.VMEM((2, 2 * tq, tk), BF16),
                        pltpu.VMEM((2, tq, LANES), F32),
                        pltpu.VMEM((2, tq, LANES), F32)],
        compiler_params=_cparams(("parallel", "parallel", "parallel")),
        name=f"attn_d{d}",
    )(_attn_bias(tq), qkv, qkv, qkv, qkv, qkv, qkv, qkv)


def _fourier_kernel(x_ref, m1_ref, m2_ref, csc_ref, o_ref, xf_ref, yr_ref, yi_ref,
                    *, n1, n2, pitch, scale):
    xf_ref[...] = x_ref[...].astype(F32)

    def stage1(m, carry):
        xm = xf_ref[pl.ds(m, n1, stride=n2), :].astype(BF16)
        y = jnp.dot(m1_ref[m], xm, preferred_element_type=F32)
        yr_ref[pl.ds(m, n1, stride=pitch), :] = y[:n1]
        yi_ref[pl.ds(m, n1, stride=pitch), :] = y[n1:]
        return carry

    lax.fori_loop(0, n2, stage1, 0, unroll=8)

    m2 = m2_ref[...]
    csc = csc_ref[...]

    def stage2(kk, carry):
        ka = 2 * kk
        kb = ka + 1
        base_a = pl.multiple_of(ka * pitch, 8)
        base_b = pl.multiple_of(kb * pitch, 8)
        dat_a = jnp.concatenate([yr_ref[pl.ds(base_a, n2), :], yi_ref[pl.ds(base_a, n2), :]], axis=0)
        dat_b = jnp.concatenate([yr_ref[pl.ds(base_b, n2), :], yi_ref[pl.ds(base_b, n2), :]], axis=0)
        dat = jnp.concatenate([dat_a, dat_b], axis=1).astype(BF16)
        o = jnp.dot(m2, dat, preferred_element_type=F32).astype(BF16)
        lhs = jnp.concatenate(
            [jnp.concatenate([o[:n2, :LANES], o[n2:, :LANES]], axis=1),
             jnp.concatenate([o[:n2, LANES:], o[n2:, LANES:]], axis=1)], axis=0)
        res = jnp.dot(lhs, csc, preferred_element_type=F32) * scale
        o_ref[pl.ds(ka, n2, stride=n1), :] = res[:n2]
        o_ref[pl.ds(kb, n2, stride=n1), :] = res[n2:]
        return carry

    lax.fori_loop(0, n1 // 2, stage2, 0, unroll=4)


def _dft_mats(n):
    k = np.arange(n)
    ang = 2.0 * np.pi * ((k[:, None] * k[None, :]) % n) / n
    return np.cos(ang), np.sin(ang)


def _fourier(f3, *, first_lane_block=0):
    batch, seq, _ = f3.shape
    ch = FOURIER_GROUP
    groups = N_FOURIER_GROUPS
    width = groups * ch
    n1 = DFT_N1
    n2 = seq // n1
    pitch = n2 + DFT_PAD
    pos = n2 * np.arange(n1)[None, None, :] + np.arange(n2)[:, None, None]
    ang = 2.0 * np.pi * ((np.arange(n1)[None, :, None] * pos) % seq) / seq
    m1 = jnp.asarray(np.concatenate([np.cos(ang), -np.sin(ang)], axis=1), F32).astype(BF16)
    c2, s2 = _dft_mats(n2)
    m2 = jnp.asarray(np.block([[c2, s2], [-s2, c2]]), F32).astype(BF16)
    cch, sch = _dft_mats(ch)
    csc = jnp.asarray(np.concatenate([cch, sch], axis=0), F32).astype(BF16)
    assert n1 % 2 == 0
    kern = functools.partial(_fourier_kernel, n1=n1, n2=n2, pitch=pitch,
                             scale=1.0 / math.sqrt(seq * ch))
    return pl.pallas_call(
        kern,
        grid=(batch, groups),
        in_specs=[
            pl.BlockSpec((None, seq, ch), lambda b, g: (b, 0, first_lane_block + g)),
            pl.BlockSpec((n2, 2 * n1, n1), lambda b, g: (0, 0, 0)),
            pl.BlockSpec((2 * n2, 2 * n2), lambda b, g: (0, 0)),
            pl.BlockSpec((2 * ch, ch), lambda b, g: (0, 0)),
        ],
        out_specs=pl.BlockSpec((None, seq, ch), lambda b, g: (b, 0, g)),
        out_shape=jax.ShapeDtypeStruct((batch, seq, width), F32),
        scratch_shapes=[pltpu.VMEM((seq, ch), F32),
                        pltpu.VMEM((n1 * pitch, ch), F32),
                        pltpu.VMEM((n1 * pitch, ch), F32)],
        compiler_params=_cparams(("parallel", "parallel")),
        name="fourier",
    )(f3, m1, m2, csc)


def _out_proj_kernel(o1_ref, o4_ref, o16_ref, l1_ref, l4_ref, l16_ref, four_ref, x_ref,
                     ga_ref, gf_ref, wa_ref, wf_ref, g2_ref, e_ref, x1_ref, h2_ref,
                     n4_ref, n16_ref, ln4_ref, ln16_ref):
    tm = x_ref.shape[0]
    n_slabs = n4_ref.shape[0]

    for r in range(4):
        blk = o4_ref[r].astype(F32)
        for c in range(n_slabs):
            n4_ref[c, pl.ds(r, tm // 4, stride=4), :] = blk[:, c * LANES:(c + 1) * LANES]
        ln4_ref[pl.ds(r, tm // 4, stride=4), :] = l4_ref[r]
    for r in range(16):
        blk = o16_ref[r].astype(F32)
        for c in range(n_slabs):
            n16_ref[c, pl.ds(r, tm // 16, stride=16), :] = blk[:, c * LANES:(c + 1) * LANES]
        ln16_ref[pl.ds(r, tm // 16, stride=16), :] = l16_ref[r]

    l1 = l1_ref[...]
    l2 = ln4_ref[...]
    l3 = ln16_ref[...]
    m = jnp.maximum(jnp.maximum(l1, l2), l3)
    e1 = jnp.exp(l1 - m)
    e2 = jnp.exp(l2 - m)
    e3 = jnp.exp(l3 - m)
    inv = 1.0 / (e1 + e2 + e3)
    expand = e_ref[...]

    def spread(w):
        hi = w.astype(BF16)
        lo = (w - hi.astype(F32)).astype(BF16)
        return jnp.dot(hi, expand, preferred_element_type=F32) \
            + jnp.dot(lo, expand, preferred_element_type=F32)

    w1 = spread(e1 * inv)
    w2 = spread(e2 * inv)
    o1 = o1_ref[...].astype(F32)
    slabs = []
    ssq = jnp.zeros((tm, 1), F32)
    for c in range(n_slabs):
        lanes = slice(c * LANES, (c + 1) * LANES)
        o3 = n16_ref[c]
        a = o3 + w1[:, lanes] * (o1[:, lanes] - o3) + w2[:, lanes] * (n4_ref[c] - o3)
        ssq = ssq + jnp.sum(a * a, axis=-1, keepdims=True)
        slabs.append(a)
    attn = jnp.concatenate(slabs, axis=1)
    width = n_slabs * LANES
    a_n = (attn * lax.rsqrt(ssq / width + EPS) * ga_ref[...]).astype(BF16)
    f_n = _rms(four_ref[...], gf_ref[...]).astype(BF16)
    y = jnp.dot(a_n, wa_ref[...], preferred_element_type=F32) \
        + jnp.dot(f_n, wf_ref[...], preferred_element_type=F32)
    x1 = x_ref[...] + y
    x1_ref[...] = x1
    h2_ref[...] = _rms(x1, g2_ref[...]).astype(BF16)


def _out_proj(o1, o4, o16, l1, l4, l16, four, x2, ga, gf, wa, wf, g2, *, seq, tm=256):
    ntok, d_model = x2.shape
    aw = o1.shape[1]
    fw = four.shape[1]
    n_heads = aw // HEAD_DIM
    tps = seq // tm
    expand = np.zeros((LANES, aw), np.float32)
    for h in range(n_heads):
        expand[(h % 2) * HEAD_DIM + h // 2, h * HEAD_DIM:(h + 1) * HEAD_DIM] = 1.0
    expand = jnp.asarray(expand).astype(BF16)
    row = lambda w: pl.BlockSpec((tm, w), lambda i: (i, 0))
    cls = lambda d, w: pl.BlockSpec((None, d, tm // d, w), lambda i: (i // tps, 0, i % tps, 0))
    const = lambda a, b: pl.BlockSpec((a, b), lambda i: (0, 0))
    return pl.pallas_call(
        _out_proj_kernel,
        grid=(ntok // tm,),
        in_specs=[row(aw), cls(4, aw), cls(16, aw), row(LANES), cls(4, LANES), cls(16, LANES),
                  row(fw), row(d_model),
                  const(1, aw), const(1, fw), const(aw, d_model), const(fw, d_model),
                  const(1, d_model), const(LANES, aw)],
        out_specs=[row(d_model), row(d_model)],
        out_shape=[jax.ShapeDtypeStruct((ntok, d_model), F32),
                   jax.ShapeDtypeStruct((ntok, d_model), BF16)],
        scratch_shapes=[pltpu.VMEM((aw // LANES, tm, LANES), F32),
                        pltpu.VMEM((aw // LANES, tm, LANES), F32),
                        pltpu.VMEM((tm, LANES), F32),
                        pltpu.VMEM((tm, LANES), F32)],
        compiler_params=_cparams(("parallel",)),
        name="out_proj",
    )(o1, o4, o16, l1, l4, l16, four, x2, ga, gf, wa, wf, g2, expand)


HALO = 16


def _up_kernel(hp_ref, hc_ref, hn_ref, wg_ref, wv_ref, cwg_ref, cwv_ref, cbg_ref, cbv_ref,
               o_ref, lhs_ref, ug_ref, uv_ref, *, tm, tiles_per_seq):
    i = pl.program_id(0)
    j = pl.program_id(1)

    @pl.when(j == 0)
    def _():
        t = i % tiles_per_seq
        zero = jnp.zeros((HALO, lhs_ref.shape[1]), BF16)
        lhs_ref[0:HALO] = jnp.where(t == 0, zero, hp_ref[...])
        lhs_ref[HALO:HALO + tm] = hc_ref[...]
        lhs_ref[HALO + tm:HALO + tm + HALO] = jnp.where(t == tiles_per_seq - 1, zero, hn_ref[...])

    lhs = lhs_ref[...]
    ug_ref[...] = jnp.dot(lhs, wg_ref[...], preferred_element_type=F32)
    uv_ref[...] = jnp.dot(lhs, wv_ref[...], preferred_element_type=F32)

    def conv(u_ref, cw_ref, cb_ref):
        cw = cw_ref[...]
        return (u_ref[HALO - 1:HALO - 1 + tm] * cw[0:1]
                + u_ref[HALO:HALO + tm] * cw[1:2]
                + u_ref[HALO + 1:HALO + 1 + tm] * cw[2:3]
                + cb_ref[...])

    gate = conv(ug_ref, cwg_ref, cbg_ref)
    val = conv(uv_ref, cwv_ref, cbv_ref)
    o_ref[...] = (gate * (1.0 / (1.0 + jnp.exp(-gate))) * val).astype(BF16)


def _up_proj(h2, w_up, conv_w, conv_b, *, seq, tm=512, tn=512):
    ntok, d_model = h2.shape
    d_ff = w_up.shape[1] // 2
    nj = d_ff // tn
    tps = seq // tm
    hb = tm // HALO
    last = ntok // HALO - 1
    kern = functools.partial(_up_kernel, tm=tm, tiles_per_seq=tps)
    return pl.pallas_call(
        kern,
        grid=(ntok // tm, nj),
        in_specs=[
            pl.BlockSpec((HALO, d_model), lambda i, j: (jnp.maximum(i * hb - 1, 0), 0)),
            pl.BlockSpec((tm, d_model), lambda i, j: (i, 0)),
            pl.BlockSpec((HALO, d_model), lambda i, j: (jnp.minimum((i + 1) * hb, last), 0)),
            pl.BlockSpec((d_model, tn), lambda i, j: (0, j)),
            pl.BlockSpec((d_model, tn), lambda i, j: (0, nj + j)),
            pl.BlockSpec((3, tn), lambda i, j: (0, j)),
            pl.BlockSpec((3, tn), lambda i, j: (0, nj + j)),
            pl.BlockSpec((1, tn), lambda i, j: (0, j)),
            pl.BlockSpec((1, tn), lambda i, j: (0, nj + j)),
        ],
        out_specs=pl.BlockSpec((tm, tn), lambda i, j: (i, j)),
        out_shape=jax.ShapeDtypeStruct((ntok, d_ff), BF16),
        scratch_shapes=[pltpu.VMEM((tm + 2 * HALO, d_model), BF16),
                        pltpu.VMEM((tm + 2 * HALO, tn), F32),
                        pltpu.VMEM((tm + 2 * HALO, tn), F32)],
        compiler_params=_cparams(("parallel", "arbitrary")),
        name="up_proj",
    )(h2, h2, h2, w_up, w_up, conv_w, conv_w, conv_b, conv_b)


def _down_kernel(a_ref, w_ref, x1_ref, g_ref, y_ref, *, tn, n_blocks):
    n = pl.program_id(1)
    y = jnp.dot(a_ref[...], w_ref[...], preferred_element_type=F32)
    for b in range(n_blocks):
        @pl.when(n == b)
        def _(b=b):
            cols = slice(b * tn, (b + 1) * tn)
            y_ref[:, cols] = x1_ref[:, cols] + y

    @pl.when(n == n_blocks - 1)
    def _():
        y_ref[...] = _rms(y_ref[...], g_ref[...])


def _down_proj(act, w_down, x1, final_g, *, tm=512, tn=512):
    ntok, d_ff = act.shape
    d_model = w_down.shape[1]
    n_blocks = d_model // tn
    kern = functools.partial(_down_kernel, tn=tn, n_blocks=n_blocks)
    return pl.pallas_call(
        kern,
        grid=(ntok // tm, n_blocks),
        in_specs=[
            pl.BlockSpec((tm, d_ff), lambda i, n: (i, 0)),
            pl.BlockSpec((d_ff, tn), lambda i, n: (0, n)),
            pl.BlockSpec((tm, d_model), lambda i, n: (i, 0)),
            pl.BlockSpec((1, d_model), lambda i, n: (0, 0)),
        ],
        out_specs=pl.BlockSpec((tm, d_model), lambda i, n: (i, 0)),
        out_shape=jax.ShapeDtypeStruct((ntok, d_model), F32),
        compiler_params=_cparams(("parallel", "arbitrary")),
        name="down_proj",
    )(act, w_down, x1, final_g)


def _rope_lane_tables(seq):
    inv_freq = ROPE_THETA ** (-jnp.arange(0, ROPE_DIM, 2, dtype=F32) / ROPE_DIM)
    ang = jnp.arange(seq, dtype=F32)[:, None] * inv_freq[None, :]
    cos, sin = jnp.cos(ang), jnp.sin(ang)
    half = ROPE_DIM // 2
    rest = HEAD_DIM - ROPE_DIM
    cos_h = jnp.concatenate([cos, cos, jnp.ones((seq, rest), F32)], axis=1)
    sa_h = jnp.concatenate([-sin, jnp.zeros((seq, HEAD_DIM - half), F32)], axis=1)
    sb_h = jnp.concatenate([jnp.zeros((seq, half), F32), sin, jnp.zeros((seq, rest), F32)], axis=1)
    reps = LANES // HEAD_DIM
    rope = jnp.stack([jnp.tile(t, (1, reps)) for t in (cos_h, sa_h, sb_h)])
    ident = jnp.stack([jnp.ones((seq, LANES), F32), jnp.zeros((seq, LANES), F32),
                       jnp.zeros((seq, LANES), F32)])
    return jnp.stack([rope, ident])


def _encoder(x, p):
    batch, seq, d_model = x.shape
    aw = p["attn_width"]
    ntok = batch * seq
    x2 = x.reshape(ntok, d_model)

    tn = N_FOURIER_GROUPS * FOURIER_GROUP
    nat, c4, c16 = _in_proj(x2, p["norm1_g"], p["w_in"], p["rope"], batch=batch, seq=seq,
                            n_rope_tiles=2 * aw // tn, tn=tn)
    nat3 = nat.reshape(batch, seq, -1)
    o1, l1 = _attention(nat3.reshape(batch, 1, seq, -1), attn_width=aw)
    o4, l4 = _attention(c4, attn_width=aw)
    o16, l16 = _attention(c16, attn_width=aw)
    four = _fourier(nat3, first_lane_block=3 * aw // FOURIER_GROUP)

    x1, h2 = _out_proj(o1.reshape(ntok, aw), o4, o16, l1.reshape(ntok, LANES), l4, l16,
                       four.reshape(ntok, -1), x2,
                       p["attn_out_g"], p["fourier_out_g"], p["w_out_a"], p["w_out_f"],
                       p["norm2_g"], seq=seq)
    act = _up_proj(h2, p["w_up"], p["conv_w"], p["conv_b"], seq=seq)
    y = _down_proj(act, p["w_down"], x1, p["final_g"])
    return y.reshape(batch, seq, d_model)


def _prepare(seq, norm1_g, w_in, attn_out_g, fourier_out_g, w_out, norm2_g, w_up, conv_w, conv_b,
             w_down, final_g):
    aw = (w_in.shape[1] - N_FOURIER_GROUPS * FOURIER_GROUP) // 3
    col_scale = jnp.where(jnp.arange(w_in.shape[1]) < aw, LOG2E / math.sqrt(HEAD_DIM), 1.0)
    w_out_b = w_out.astype(BF16)
    return {
        "attn_width": aw,
        "rope": _rope_lane_tables(seq),
        "norm1_g": norm1_g.reshape(1, -1),
        "w_in": (w_in * col_scale[None, :].astype(F32)).astype(BF16),
        "attn_out_g": attn_out_g.reshape(1, -1),
        "fourier_out_g": fourier_out_g.reshape(1, -1),
        "w_out_a": w_out_b[:aw],
        "w_out_f": w_out_b[aw:],
        "norm2_g": norm2_g.reshape(1, -1),
        "w_up": w_up.astype(BF16),
        "conv_w": conv_w,
        "conv_b": conv_b.reshape(1, -1),
        "w_down": w_down.astype(BF16),
        "final_g": final_g.reshape(1, -1),
    }


def kernel(x_prompt, x_sample, norm1_g, w_in, attn_out_g, fourier_out_g, w_out, norm2_g, w_up,
           conv_w, conv_b, w_down, final_g):
    assert norm1_g.shape[0] == 1, "single-layer encoder"
    assert x_prompt.shape[1] == x_sample.shape[1]
    p = _prepare(x_prompt.shape[1], norm1_g[0], w_in[0], attn_out_g[0], fourier_out_g[0], w_out[0],
                 norm2_g[0], w_up[0], conv_w[0], conv_b[0], w_down[0], final_g)
    return (_encoder(x_prompt, p), _encoder(x_sample, p))
```

```python
import functools
import math

import numpy as np
import jax
import jax.numpy as jnp
from jax import lax
from jax.experimental import pallas as pl
from jax.experimental.pallas import tpu as pltpu

F32 = jnp.float32
BF16 = jnp.bfloat16

HEAD_DIM = 64
ROPE_DIM = 16
ROPE_THETA = 500000.0
N_FOURIER_GROUPS = 4
FOURIER_GROUP = 128
DILATIONS = (1, 4, 16)
BAND = 64
EPS = 1e-6
MASK_VALUE = -1e30
LOG2E = 1.4426950408889634
LN2 = 0.6931471805599453

LANES = 128
VMEM_LIMIT_BYTES = 56 * 1024 * 1024

DFT_N1 = 64
DFT_PAD = 8


def _cparams(semantics):
    return pltpu.CompilerParams(dimension_semantics=semantics,
                                vmem_limit_bytes=VMEM_LIMIT_BYTES)


def _rms(x, g):
    var = jnp.mean(x * x, axis=-1, keepdims=True)
    return x * lax.rsqrt(var + EPS) * g


def _in_proj_kernel(x_ref, g_ref, w_ref, tab_ref, nat_ref, c4_ref, c16_ref,
                    h_ref, acc_ref, s1_ref, s2_ref, *, nj):
    n = pl.program_id(0)
    slot = n % 2
    tm = acc_ref.shape[1]
    n_chunks = acc_ref.shape[2] // LANES

    @pl.when(n == 0)
    def _():
        acc_ref[1] = jnp.zeros(acc_ref.shape[1:], F32)

    @pl.when(n % nj == 0)
    def _():
        h_ref[...] = _rms(x_ref[...], g_ref[...]).astype(BF16)

    cos = tab_ref[0]
    sa = tab_ref[1]
    sb = tab_ref[2]
    for c in range(n_chunks):
        lanes = slice(c * LANES, (c + 1) * LANES)
        a = acc_ref[1 - slot, :, lanes]
        r = a * cos + pltpu.roll(a, LANES - ROPE_DIM // 2, 1) * sa \
            + pltpu.roll(a, ROPE_DIM // 2, 1) * sb
        nat_ref[:, lanes] = r.astype(BF16)
        s1_ref[c] = r
        for r4 in range(4):
            x4 = s1_ref[c, pl.ds(r4, tm // 4, stride=4), :]
            c4_ref[r4, :, lanes] = x4.astype(BF16)
            s2_ref[c, r4] = x4
            for q4 in range(4):
                x16 = s2_ref[c, r4, pl.ds(q4, tm // 16, stride=4), :]
                c16_ref[r4 + 4 * q4, :, lanes] = x16.astype(BF16)

    acc_ref[slot] = jnp.dot(h_ref[...], w_ref[...], preferred_element_type=F32)


def _in_proj(x2, g, w, tabs, *, batch, seq, n_rope_tiles, tm=1024, tn=512):
    ntok, d_model = x2.shape
    width = w.shape[1]
    nj = width // tn
    ntiles = ntok // tm
    assert tm % 256 == 0
    tps = seq // tm
    kern = functools.partial(_in_proj_kernel, nj=nj)

    def prev(n):
        pn = jnp.maximum(n - 1, 0)
        return pn // nj, pn % nj

    def tab_map(n):
        pi, pj = prev(n)
        return (jnp.where(pj < n_rope_tiles, 0, 1), 0, pi % tps, 0)

    def nat_map(n):
        return prev(n)

    def cls_map(n):
        pi, pj = prev(n)
        return (pi // tps, 0, pi % tps, pj)

    return pl.pallas_call(
        kern,
        grid=(ntiles * nj + 1,),
        in_specs=[
            pl.BlockSpec((tm, d_model), lambda n: (jnp.minimum(n // nj, ntiles - 1), 0)),
            pl.BlockSpec((1, d_model), lambda n: (0, 0)),
            pl.BlockSpec((d_model, tn), lambda n: (0, n % nj)),
            pl.BlockSpec((None, 3, tm, LANES), tab_map),
        ],
        out_specs=[
            pl.BlockSpec((tm, tn), nat_map),
            pl.BlockSpec((None, 4, tm // 4, tn), cls_map),
            pl.BlockSpec((None, 16, tm // 16, tn), cls_map),
        ],
        out_shape=[
            jax.ShapeDtypeStruct((ntok, width), BF16),
            jax.ShapeDtypeStruct((batch, 4, seq // 4, width), BF16),
            jax.ShapeDtypeStruct((batch, 16, seq // 16, width), BF16),
        ],
        scratch_shapes=[pltpu.VMEM((tm, d_model), BF16),
                        pltpu.VMEM((2, tm, tn), F32),
                        pltpu.VMEM((tn // LANES, tm, LANES), F32),
                        pltpu.VMEM((tn // LANES, 4, tm // 4, LANES), F32)],
        compiler_params=_cparams(("arbitrary",)),
        name="in_proj",
    )(x2, g, w, tabs)


def _attn_kernel(bias_ref, q_ref, kp_ref, kc_ref, kn_ref, vp_ref, vc_ref, vn_ref,
                 o_ref, lse_ref, s_ref, p_ref, m_ref, d_ref,
                 *, tb, tq, n_chunks, pairs, row_chunk=32):
    j = pl.program_id(2)
    tk = tq + 2 * BAND
    n_sub = tb // tq

    lane = lax.broadcasted_iota(jnp.int32, (tq, LANES), 1)
    first_head = lane < HEAD_DIM
    first_chunk = lax.broadcasted_iota(jnp.int32, (row_chunk, LANES), 1) < HEAD_DIM

    def window(prev_ref, cur_ref, next_ref, i, lanes):
        if i == 0:
            return jnp.concatenate([prev_ref[:, lanes], cur_ref[0:tq + BAND, lanes]], axis=0)
        if i == n_sub - 1:
            return jnp.concatenate([cur_ref[tb - tq - BAND:tb, lanes], next_ref[:, lanes]], axis=0)
        return cur_ref[i * tq - BAND:(i + 1) * tq + BAND, lanes]

    units = [(i, p) for i in range(n_sub) for p in range(pairs)]

    def stage_a(u):
        i, p = units[u]
        slot = u % 2
        lanes = slice(p * LANES, (p + 1) * LANES)
        q = q_ref[i * tq:(i + 1) * tq, lanes]
        zero = jnp.zeros_like(q)
        q2 = jnp.concatenate([jnp.where(first_head, q, zero), jnp.where(first_head, zero, q)],
                             axis=0)
        kw = window(kp_ref, kc_ref, kn_ref, i, lanes)
        s = lax.dot_general(q2, kw, (((1,), (1,)), ((), ())), preferred_element_type=F32)
        if i == 0:
            idx = jnp.where(j == 0, 1, 0)
        elif i == n_sub - 1:
            idx = jnp.where(j == n_chunks - 1, 2, 0)
        else:
            idx = 0
        s_ref[slot] = s + bias_ref[idx]

    def stage_b(u):
        slot = u % 2
        for r in range(0, tq, row_chunk):
            s0 = s_ref[slot, r:r + row_chunk, :]
            s1 = s_ref[slot, tq + r:tq + r + row_chunk, :]
            m0 = jnp.max(s0, axis=-1, keepdims=True)
            m1 = jnp.max(s1, axis=-1, keepdims=True)
            e0 = jnp.exp2(s0 - m0)
            e1 = jnp.exp2(s1 - m1)
            p_ref[slot, r:r + row_chunk, :] = e0.astype(BF16)
            p_ref[slot, tq + r:tq + r + row_chunk, :] = e1.astype(BF16)
            m_ref[slot, r:r + row_chunk, :] = jnp.where(
                first_chunk, jnp.broadcast_to(m0, (row_chunk, LANES)),
                jnp.broadcast_to(m1, (row_chunk, LANES)))
            d0 = jnp.sum(e0, axis=-1, keepdims=True)
            d1 = jnp.sum(e1, axis=-1, keepdims=True)
            d_ref[slot, r:r + row_chunk, :] = jnp.where(
                first_chunk, jnp.broadcast_to(d0, (row_chunk, LANES)),
                jnp.broadcast_to(d1, (row_chunk, LANES)))

    def stage_c(u):
        i, p = units[u]
        slot = u % 2
        lanes = slice(p * LANES, (p + 1) * LANES)
        pv = jnp.dot(p_ref[slot], window(vp_ref, vc_ref, vn_ref, i, lanes),
                     preferred_element_type=F32)
        o = jnp.where(first_head, pv[:tq], pv[tq:])
        dens = d_ref[slot]
        o_ref[i * tq:(i + 1) * tq, lanes] = (o * (1.0 / dens)).astype(BF16)
        lse = m_ref[slot] * LN2 + jnp.log(dens)
        cur = jnp.zeros((tq, LANES), F32) if p == 0 else lse_ref[i * tq:(i + 1) * tq, :]
        lse_ref[i * tq:(i + 1) * tq, :] = jnp.where(
            jnp.bitwise_and(lane, HEAD_DIM - 1) == p, lse, cur)

    n_units = len(units)
    for t in range(n_units + 2):
        if t < n_units:
            stage_a(t)
        if 0 <= t - 1 < n_units:
            stage_b(t - 1)
        if 0 <= t - 2 < n_units:
            stage_c(t - 2)


def _attn_bias(tq):
    tk = tq + 2 * BAND
    col = np.arange(tk)[None, :]
    row = np.arange(tq)[:, None]
    band = (col - row >= 0) & (col - row <= 2 * BAND)
    variants = [band, band & (col >= BAND), band & (col < tq + BAND)]
    bias = np.stack([np.where(v, 0.0, MASK_VALUE) for v in variants]).astype(np.float32)
    return jnp.asarray(np.concatenate([bias, bias], axis=1))


def _attention(qkv, *, attn_width, tq=128, max_tb=512):
    batch, d, t_len, width = qkv.shape
    tb = min(max_tb, t_len)
    assert t_len % tb == 0 and tb % tq == 0 and tb // tq >= 2
    n_chunks = t_len // tb
    pairs = attn_width // LANES
    hb = tb // BAND
    last_hb = t_len // BAND - 1
    tk = tq + 2 * BAND

    def cur(off):
        return pl.BlockSpec((None, None, tb, attn_width), lambda b, r, j: (b, r, j, off))

    def prv(off):
        return pl.BlockSpec((None, None, BAND, attn_width),
                            lambda b, r, j: (b, r, jnp.maximum(j * hb - 1, 0), off))

    def nxt(off):
        return pl.BlockSpec((None, None, BAND, attn_width),
                            lambda b, r, j: (b, r, jnp.minimum((j + 1) * hb, last_hb), off))

    kern = functools.partial(_attn_kernel, tb=tb, tq=tq, n_chunks=n_chunks, pairs=pairs)
    return pl.pallas_call(
        kern,
        grid=(batch, d, n_chunks),
        in_specs=[pl.BlockSpec((3, 2 * tq, tk), lambda b, r, j: (0, 0, 0)),
                  cur(0), prv(1), cur(1), nxt(1), prv(2), cur(2), nxt(2)],
        out_specs=[
            pl.BlockSpec((None, None, tb, attn_width), lambda b, r, j: (b, r, j, 0)),
            pl.BlockSpec((None, None, tb, LANES), lambda b, r, j: (b, r, j, 0)),
        ],
        out_shape=[
            jax.ShapeDtypeStruct((batch, d, t_len, attn_width), BF16),
            jax.ShapeDtypeStruct((batch, d, t_len, LANES), F32),
        ],
        scratch_shapes=[pltpu.VMEM((2, 2 * tq, tk), F32),
                        pltpu.VMEM((2, 2 * tq, tk), BF16),
                        pltpu.VMEM((2, tq, LANES), F32),
                        pltpu.VMEM((2, tq, LANES), F32)],
        compiler_params=_cparams(("parallel", "parallel", "parallel")),
        name=f"attn_d{d}",
    )(_attn_bias(tq), qkv, qkv, qkv, qkv, qkv, qkv, qkv)


def _fourier_kernel(x_ref, m1_ref, m2_ref, csc_ref, o_ref, xf_ref, yr_ref, yi_ref,
                    *, n1, n2, pitch, scale):
    xf_ref[...] = x_ref[...].astype(F32)

    def stage1(m, carry):
        xm = xf_ref[pl.ds(m, n1, stride=n2), :].astype(BF16)
        y = jnp.dot(m1_ref[m], xm, preferred_element_type=F32)
        yr_ref[pl.ds(m, n1, stride=pitch), :] = y[:n1]
        yi_ref[pl.ds(m, n1, stride=pitch), :] = y[n1:]
        return carry

    lax.fori_loop(0, n2, stage1, 0, unroll=8)

    m2 = m2_ref[...]
    csc = csc_ref[...]

    def stage2(kk, carry):
        ka = 2 * kk
        kb = ka + 1
        base_a = pl.multiple_of(ka * pitch, 8)
        base_b = pl.multiple_of(kb * pitch, 8)
        dat_a = jnp.concatenate([yr_ref[pl.ds(base_a, n2), :], yi_ref[pl.ds(base_a, n2), :]], axis=0)
        dat_b = jnp.concatenate([yr_ref[pl.ds(base_b, n2), :], yi_ref[pl.ds(base_b, n2), :]], axis=0)
        dat = jnp.concatenate([dat_a, dat_b], axis=1).astype(BF16)
        o = jnp.dot(m2, dat, preferred_element_type=F32).astype(BF16)
        lhs = jnp.concatenate(
            [jnp.concatenate([o[:n2, :LANES], o[n2:, :LANES]], axis=1),
             jnp.concatenate([o[:n2, LANES:], o[n2:, LANES:]], axis=1)], axis=0)
        res = jnp.dot(lhs, csc, preferred_element_type=F32) * scale
        o_ref[pl.ds(ka, n2, stride=n1), :] = res[:n2]
        o_ref[pl.ds(kb, n2, stride=n1), :] = res[n2:]
        return carry

    lax.fori_loop(0, n1 // 2, stage2, 0, unroll=4)


def _dft_mats(n):
    k = np.arange(n)
    ang = 2.0 * np.pi * ((k[:, None] * k[None, :]) % n) / n
    return np.cos(ang), np.sin(ang)


def _fourier(f3, *, first_lane_block=0):
    batch, seq, _ = f3.shape
    ch = FOURIER_GROUP
    groups = N_FOURIER_GROUPS
    width = groups * ch
    n1 = DFT_N1
    n2 = seq // n1
    pitch = n2 + DFT_PAD
    pos = n2 * np.arange(n1)[None, None, :] + np.arange(n2)[:, None, None]
    ang = 2.0 * np.pi * ((np.arange(n1)[None, :, None] * pos) % seq) / seq
    m1 = jnp.asarray(np.concatenate([np.cos(ang), -np.sin(ang)], axis=1), F32).astype(BF16)
    c2, s2 = _dft_mats(n2)
    m2 = jnp.asarray(np.block([[c2, s2], [-s2, c2]]), F32).astype(BF16)
    cch, sch = _dft_mats(ch)
    csc = jnp.asarray(np.concatenate([cch, sch], axis=0), F32).astype(BF16)
    assert n1 % 2 == 0
    kern = functools.partial(_fourier_kernel, n1=n1, n2=n2, pitch=pitch,
                             scale=1.0 / math.sqrt(seq * ch))
    return pl.pallas_call(
        kern,
        grid=(batch, groups),
        in_specs=[
            pl.BlockSpec((None, seq, ch), lambda b, g: (b, 0, first_lane_block + g)),
            pl.BlockSpec((n2, 2 * n1, n1), lambda b, g: (0, 0, 0)),
            pl.BlockSpec((2 * n2, 2 * n2), lambda b, g: (0, 0)),
            pl.BlockSpec((2 * ch, ch), lambda b, g: (0, 0)),
        ],
        out_specs=pl.BlockSpec((None, seq, ch), lambda b, g: (b, 0, g)),
        out_shape=jax.ShapeDtypeStruct((batch, seq, width), F32),
        scratch_shapes=[pltpu.VMEM((seq, ch), F32),
                        pltpu.VMEM((n1 * pitch, ch), F32),
                        pltpu.VMEM((n1 * pitch, ch), F32)],
        compiler_params=_cparams(("parallel", "parallel")),
        name="fourier",
    )(f3, m1, m2, csc)


def _out_proj_kernel(o1_ref, o4_ref, o16_ref, l1_ref, l4_ref, l16_ref, four_ref, x_ref,
                     ga_ref, gf_ref, wa_ref, wf_ref, g2_ref, e_ref, x1_ref, h2_ref,
                     n4_ref, n16_ref, ln4_ref, ln16_ref):
    tm = x_ref.shape[0]
    n_slabs = n4_ref.shape[0]

    for r in range(4):
        blk = o4_ref[r].astype(F32)
        for c in range(n_slabs):
            n4_ref[c, pl.ds(r, tm // 4, stride=4), :] = blk[:, c * LANES:(c + 1) * LANES]
        ln4_ref[pl.ds(r, tm // 4, stride=4), :] = l4_ref[r]
    for r in range(16):
        blk = o16_ref[r].astype(F32)
        for c in range(n_slabs):
            n16_ref[c, pl.ds(r, tm // 16, stride=16), :] = blk[:, c * LANES:(c + 1) * LANES]
        ln16_ref[pl.ds(r, tm // 16, stride=16), :] = l16_ref[r]

    l1 = l1_ref[...]
    l2 = ln4_ref[...]
    l3 = ln16_ref[...]
    m = jnp.maximum(jnp.maximum(l1, l2), l3)
    e1 = jnp.exp(l1 - m)
    e2 = jnp.exp(l2 - m)
    e3 = jnp.exp(l3 - m)
    inv = 1.0 / (e1 + e2 + e3)
    expand = e_ref[...]

    def spread(w):
        hi = w.astype(BF16)
        lo = (w - hi.astype(F32)).astype(BF16)
        return jnp.dot(hi, expand, preferred_element_type=F32) \
            + jnp.dot(lo, expand, preferred_element_type=F32)

    w1 = spread(e1 * inv)
    w2 = spread(e2 * inv)
    o1 = o1_ref[...].astype(F32)
    slabs = []
    ssq = jnp.zeros((tm, 1), F32)
    for c in range(n_slabs):
        lanes = slice(c * LANES, (c + 1) * LANES)
        o3 = n16_ref[c]
        a = o3 + w1[:, lanes] * (o1[:, lanes] - o3) + w2[:, lanes] * (n4_ref[c] - o3)
        ssq = ssq + jnp.sum(a * a, axis=-1, keepdims=True)
        slabs.append(a)
    attn = jnp.concatenate(slabs, axis=1)
    width = n_slabs * LANES
    a_n = (attn * lax.rsqrt(ssq / width + EPS) * ga_ref[...]).astype(BF16)
    f_n = _rms(four_ref[...], gf_ref[...]).astype(BF16)
    y = jnp.dot(a_n, wa_ref[...], preferred_element_type=F32) \
        + jnp.dot(f_n, wf_ref[...], preferred_element_type=F32)
    x1 = x_ref[...] + y
    x1_ref[...] = x1
    h2_ref[...] = _rms(x1, g2_ref[...]).astype(BF16)


def _out_proj(o1, o4, o16, l1, l4, l16, four, x2, ga, gf, wa, wf, g2, *, seq, tm=256):
    ntok, d_model = x2.shape
    aw = o1.shape[1]
    fw = four.shape[1]
    n_heads = aw // HEAD_DIM
    tps = seq // tm
    expand = np.zeros((LANES, aw), np.float32)
    for h in range(n_heads):
        expand[(h % 2) * HEAD_DIM + h // 2, h * HEAD_DIM:(h + 1) * HEAD_DIM] = 1.0
    expand = jnp.asarray(expand).astype(BF16)
    row = lambda w: pl.BlockSpec((tm, w), lambda i: (i, 0))
    cls = lambda d, w: pl.BlockSpec((None, d, tm // d, w), lambda i: (i // tps, 0, i % tps, 0))
    const = lambda a, b: pl.BlockSpec((a, b), lambda i: (0, 0))
    return pl.pallas_call(
        _out_proj_kernel,
        grid=(ntok // tm,),
        in_specs=[row(aw), cls(4, aw), cls(16, aw), row(LANES), cls(4, LANES), cls(16, LANES),
                  row(fw), row(d_model),
                  const(1, aw), const(1, fw), const(aw, d_model), const(fw, d_model),
                  const(1, d_model), const(LANES, aw)],
        out_specs=[row(d_model), row(d_model)],
        out_shape=[jax.ShapeDtypeStruct((ntok, d_model), F32),
                   jax.ShapeDtypeStruct((ntok, d_model), BF16)],
        scratch_shapes=[pltpu.VMEM((aw // LANES, tm, LANES), F32),
                        pltpu.VMEM((aw // LANES, tm, LANES), F32),
                        pltpu.VMEM((tm, LANES), F32),
                        pltpu.VMEM((tm, LANES), F32)],
        compiler_params=_cparams(("parallel",)),
        name="out_proj",
    )(o1, o4, o16, l1, l4, l16, four, x2, ga, gf, wa, wf, g2, expand)


HALO = 16


def _up_kernel(hp_ref, hc_ref, hn_ref, wg_ref, wv_ref, cwg_ref, cwv_ref, cbg_ref, cbv_ref,
               o_ref, lhs_ref, ug_ref, uv_ref, *, tm, tiles_per_seq):
    i = pl.program_id(0)
    j = pl.program_id(1)

    @pl.when(j == 0)
    def _():
        t = i % tiles_per_seq
        zero = jnp.zeros((HALO, lhs_ref.shape[1]), BF16)
        lhs_ref[0:HALO] = jnp.where(t == 0, zero, hp_ref[...])
        lhs_ref[HALO:HALO + tm] = hc_ref[...]
        lhs_ref[HALO + tm:HALO + tm + HALO] = jnp.where(t == tiles_per_seq - 1, zero, hn_ref[...])

    lhs = lhs_ref[...]
    ug_ref[...] = jnp.dot(lhs, wg_ref[...], preferred_element_type=F32)
    uv_ref[...] = jnp.dot(lhs, wv_ref[...], preferred_element_type=F32)

    def conv(u_ref, cw_ref, cb_ref):
        cw = cw_ref[...]
        return (u_ref[HALO - 1:HALO - 1 + tm] * cw[0:1]
                + u_ref[HALO:HALO + tm] * cw[1:2]
                + u_ref[HALO + 1:HALO + 1 + tm] * cw[2:3]
                + cb_ref[...])

    gate = conv(ug_ref, cwg_ref, cbg_ref)
    val = conv(uv_ref, cwv_ref, cbv_ref)
    o_ref[...] = (gate * (1.0 / (1.0 + jnp.exp(-gate))) * val).astype(BF16)


def _up_proj(h2, w_up, conv_w, conv_b, *, seq, tm=512, tn=512):
    ntok, d_model = h2.shape
    d_ff = w_up.shape[1] // 2
    nj = d_ff // tn
    tps = seq // tm
    hb = tm // HALO
    last = ntok // HALO - 1
    kern = functools.partial(_up_kernel, tm=tm, tiles_per_seq=tps)
    return pl.pallas_call(
        kern,
        grid=(ntok // tm, nj),
        in_specs=[
            pl.BlockSpec((HALO, d_model), lambda i, j: (jnp.maximum(i * hb - 1, 0), 0)),
            pl.BlockSpec((tm, d_model), lambda i, j: (i, 0)),
            pl.BlockSpec((HALO, d_model), lambda i, j: (jnp.minimum((i + 1) * hb, last), 0)),
            pl.BlockSpec((d_model, tn), lambda i, j: (0, j)),
            pl.BlockSpec((d_model, tn), lambda i, j: (0, nj + j)),
            pl.BlockSpec((3, tn), lambda i, j: (0, j)),
            pl.BlockSpec((3, tn), lambda i, j: (0, nj + j)),
            pl.BlockSpec((1, tn), lambda i, j: (0, j)),
            pl.BlockSpec((1, tn), lambda i, j: (0, nj + j)),
        ],
        out_specs=pl.BlockSpec((tm, tn), lambda i, j: (i, j)),
        out_shape=jax.ShapeDtypeStruct((ntok, d_ff), BF16),
        scratch_shapes=[pltpu.VMEM((tm + 2 * HALO, d_model), BF16),
                        pltpu.VMEM((tm + 2 * HALO, tn), F32),
                        pltpu.VMEM((tm + 2 * HALO, tn), F32)],
        compiler_params=_cparams(("parallel", "arbitrary")),
        name="up_proj",
    )(h2, h2, h2, w_up, w_up, conv_w, conv_w, conv_b, conv_b)


def _down_kernel(a_ref, w_ref, x1_ref, g_ref, y_ref, *, tn, n_blocks):
    n = pl.program_id(1)
    y = jnp.dot(a_ref[...], w_ref[...], preferred_element_type=F32)
    for b in range(n_blocks):
        @pl.when(n == b)
        def _(b=b):
            cols = slice(b * tn, (b + 1) * tn)
            y_ref[:, cols] = x1_ref[...] + y

    @pl.when(n == n_blocks - 1)
    def _():
        y_ref[...] = _rms(y_ref[...], g_ref[...])


def _down_proj(act, w_down, x1, final_g, *, tm=1024, tn=256):
    ntok, d_ff = act.shape
    d_model = w_down.shape[1]
    n_blocks = d_model // tn
    kern = functools.partial(_down_kernel, tn=tn, n_blocks=n_blocks)
    return pl.pallas_call(
        kern,
        grid=(ntok // tm, n_blocks),
        in_specs=[
            pl.BlockSpec((tm, d_ff), lambda i, n: (i, 0)),
            pl.BlockSpec((d_ff, tn), lambda i, n: (0, n)),
            pl.BlockSpec((tm, tn), lambda i, n: (i, n)),
            pl.BlockSpec((1, d_model), lambda i, n: (0, 0)),
        ],
        out_specs=pl.BlockSpec((tm, d_model), lambda i, n: (i, 0)),
        out_shape=jax.ShapeDtypeStruct((ntok, d_model), F32),
        compiler_params=_cparams(("parallel", "arbitrary")),
        name="down_proj",
    )(act, w_down, x1, final_g)


def _rope_lane_tables(seq):
    inv_freq = ROPE_THETA ** (-jnp.arange(0, ROPE_DIM, 2, dtype=F32) / ROPE_DIM)
    ang = jnp.arange(seq, dtype=F32)[:, None] * inv_freq[None, :]
    cos, sin = jnp.cos(ang), jnp.sin(ang)
    half = ROPE_DIM // 2
    rest = HEAD_DIM - ROPE_DIM
    cos_h = jnp.concatenate([cos, cos, jnp.ones((seq, rest), F32)], axis=1)
    sa_h = jnp.concatenate([-sin, jnp.zeros((seq, HEAD_DIM - half), F32)], axis=1)
    sb_h = jnp.concatenate([jnp.zeros((seq, half), F32), sin, jnp.zeros((seq, rest), F32)], axis=1)
    reps = LANES // HEAD_DIM
    rope = jnp.stack([jnp.tile(t, (1, reps)) for t in (cos_h, sa_h, sb_h)])
    ident = jnp.stack([jnp.ones((seq, LANES), F32), jnp.zeros((seq, LANES), F32),
                       jnp.zeros((seq, LANES), F32)])
    return jnp.stack([rope, ident])


def _encoder(x, p):
    batch, seq, d_model = x.shape
    aw = p["attn_width"]
    ntok = batch * seq
    x2 = x.reshape(ntok, d_model)

    tn = N_FOURIER_GROUPS * FOURIER_GROUP
    nat, c4, c16 = _in_proj(x2, p["norm1_g"], p["w_in"], p["rope"], batch=batch, seq=seq,
                            n_rope_tiles=2 * aw // tn, tn=tn)
    nat3 = nat.reshape(batch, seq, -1)
    o1, l1 = _attention(nat3.reshape(batch, 1, seq, -1), attn_width=aw)
    o4, l4 = _attention(c4, attn_width=aw)
    o16, l16 = _attention(c16, attn_width=aw)
    four = _fourier(nat3, first_lane_block=3 * aw // FOURIER_GROUP)

    x1, h2 = _out_proj(o1.reshape(ntok, aw), o4, o16, l1.reshape(ntok, LANES), l4, l16,
                       four.reshape(ntok, -1), x2,
                       p["attn_out_g"], p["fourier_out_g"], p["w_out_a"], p["w_out_f"],
                       p["norm2_g"], seq=seq)
    act = _up_proj(h2, p["w_up"], p["conv_w"], p["conv_b"], seq=seq)
    y = _down_proj(act, p["w_down"], x1, p["final_g"])
    return y.reshape(batch, seq, d_model)


def _prepare(seq, norm1_g, w_in, attn_out_g, fourier_out_g, w_out, norm2_g, w_up, conv_w, conv_b,
             w_down, final_g):
    aw = (w_in.shape[1] - N_FOURIER_GROUPS * FOURIER_GROUP) // 3
    col_scale = jnp.where(jnp.arange(w_in.shape[1]) < aw, LOG2E / math.sqrt(HEAD_DIM), 1.0)
    w_out_b = w_out.astype(BF16)
    return {
        "attn_width": aw,
        "rope": _rope_lane_tables(seq),
        "norm1_g": norm1_g.reshape(1, -1),
        "w_in": (w_in * col_scale[None, :].astype(F32)).astype(BF16),
        "attn_out_g": attn_out_g.reshape(1, -1),
        "fourier_out_g": fourier_out_g.reshape(1, -1),
        "w_out_a": w_out_b[:aw],
        "w_out_f": w_out_b[aw:],
        "norm2_g": norm2_g.reshape(1, -1),
        "w_up": w_up.astype(BF16),
        "conv_w": conv_w,
        "conv_b": conv_b.reshape(1, -1),
        "w_down": w_down.astype(BF16),
        "final_g": final_g.reshape(1, -1),
    }


def kernel(x_prompt, x_sample, norm1_g, w_in, attn_out_g, fourier_out_g, w_out, norm2_g, w_up,
           conv_w, conv_b, w_down, final_g):
    assert norm1_g.shape[0] == 1, "single-layer encoder"
    assert x_prompt.shape[1] == x_sample.shape[1]
    p = _prepare(x_prompt.shape[1], norm1_g[0], w_in[0], attn_out_g[0], fourier_out_g[0], w_out[0],
                 norm2_g[0], w_up[0], conv_w[0], conv_b[0], w_down[0], final_g)
    return (_encoder(x_prompt, p), _encoder(x_sample, p))
```

```python
import functools
import math

import numpy as np
import jax
import jax.numpy as jnp
from jax import lax
from jax.experimental import pallas as pl
from jax.experimental.pallas import tpu as pltpu

F32 = jnp.float32
BF16 = jnp.bfloat16

HEAD_DIM = 64
ROPE_DIM = 16
ROPE_THETA = 500000.0
N_FOURIER_GROUPS = 4
FOURIER_GROUP = 128
DILATIONS = (1, 4, 16)
BAND = 64
EPS = 1e-6
MASK_VALUE = -1e30
LOG2E = 1.4426950408889634
LN2 = 0.6931471805599453

LANES = 128
VMEM_LIMIT_BYTES = 56 * 1024 * 1024

DFT_N1 = 64
DFT_PAD = 8


def _cparams(semantics):
    return pltpu.CompilerParams(dimension_semantics=semantics,
                                vmem_limit_bytes=VMEM_LIMIT_BYTES)


def _rms(x, g):
    var = jnp.mean(x * x, axis=-1, keepdims=True)
    return x * lax.rsqrt(var + EPS) * g


def _in_proj_kernel(x_ref, g_ref, w_ref, tab_ref, nat_ref, c4_ref, c16_ref,
                    h_ref, acc_ref, s1_ref, s2_ref, *, nj):
    n = pl.program_id(0)
    slot = n % 2
    tm = acc_ref.shape[1]
    n_chunks = acc_ref.shape[2] // LANES

    @pl.when(n == 0)
    def _():
        acc_ref[1] = jnp.zeros(acc_ref.shape[1:], F32)

    @pl.when(n % nj == 0)
    def _():
        h_ref[...] = _rms(x_ref[...], g_ref[...]).astype(BF16)

    cos = tab_ref[0]
    sa = tab_ref[1]
    sb = tab_ref[2]
    for c in range(n_chunks):
        lanes = slice(c * LANES, (c + 1) * LANES)
        a = acc_ref[1 - slot, :, lanes]
        r = a * cos + pltpu.roll(a, LANES - ROPE_DIM // 2, 1) * sa \
            + pltpu.roll(a, ROPE_DIM // 2, 1) * sb
        nat_ref[:, lanes] = r.astype(BF16)
        s1_ref[c] = r
        for r4 in range(4):
            x4 = s1_ref[c, pl.ds(r4, tm // 4, stride=4), :]
            c4_ref[r4, :, lanes] = x4.astype(BF16)
            s2_ref[c, r4] = x4
            for q4 in range(4):
                x16 = s2_ref[c, r4, pl.ds(q4, tm // 16, stride=4), :]
                c16_ref[r4 + 4 * q4, :, lanes] = x16.astype(BF16)

    acc_ref[slot] = jnp.dot(h_ref[...], w_ref[...], preferred_element_type=F32)


def _in_proj(x2, g, w, tabs, *, batch, seq, n_rope_tiles, tm=1024, tn=512):
    ntok, d_model = x2.shape
    width = w.shape[1]
    nj = width // tn
    ntiles = ntok // tm
    assert tm % 256 == 0
    tps = seq // tm
    kern = functools.partial(_in_proj_kernel, nj=nj)

    def prev(n):
        pn = jnp.maximum(n - 1, 0)
        return pn // nj, pn % nj

    def tab_map(n):
        pi, pj = prev(n)
        return (jnp.where(pj < n_rope_tiles, 0, 1), 0, pi % tps, 0)

    def nat_map(n):
        return prev(n)

    def cls_map(n):
        pi, pj = prev(n)
        return (pi // tps, 0, pi % tps, pj)

    return pl.pallas_call(
        kern,
        grid=(ntiles * nj + 1,),
        in_specs=[
            pl.BlockSpec((tm, d_model), lambda n: (jnp.minimum(n // nj, ntiles - 1), 0)),
            pl.BlockSpec((1, d_model), lambda n: (0, 0)),
            pl.BlockSpec((d_model, tn), lambda n: (0, n % nj)),
            pl.BlockSpec((None, 3, tm, LANES), tab_map),
        ],
        out_specs=[
            pl.BlockSpec((tm, tn), nat_map),
            pl.BlockSpec((None, 4, tm // 4, tn), cls_map),
            pl.BlockSpec((None, 16, tm // 16, tn), cls_map),
        ],
        out_shape=[
            jax.ShapeDtypeStruct((ntok, width), BF16),
            jax.ShapeDtypeStruct((batch, 4, seq // 4, width), BF16),
            jax.ShapeDtypeStruct((batch, 16, seq // 16, width), BF16),
        ],
        scratch_shapes=[pltpu.VMEM((tm, d_model), BF16),
                        pltpu.VMEM((2, tm, tn), F32),
                        pltpu.VMEM((tn // LANES, tm, LANES), F32),
                        pltpu.VMEM((tn // LANES, 4, tm // 4, LANES), F32)],
        compiler_params=_cparams(("arbitrary",)),
        name="in_proj",
    )(x2, g, w, tabs)


def _attn_kernel(bias_ref, q_ref, kp_ref, kc_ref, kn_ref, vp_ref, vc_ref, vn_ref,
                 o_ref, lse_ref, s_ref, p_ref, m_ref, d_ref,
                 *, tb, tq, n_chunks, pairs, row_chunk=16):
    j = pl.program_id(2)
    tk = tq + 2 * BAND
    n_sub = tb // tq

    lane = lax.broadcasted_iota(jnp.int32, (tq, LANES), 1)
    first_head = lane < HEAD_DIM
    first_chunk = lax.broadcasted_iota(jnp.int32, (row_chunk, LANES), 1) < HEAD_DIM

    def window(prev_ref, cur_ref, next_ref, i, lanes):
        if i == 0:
            return jnp.concatenate([prev_ref[:, lanes], cur_ref[0:tq + BAND, lanes]], axis=0)
        if i == n_sub - 1:
            return jnp.concatenate([cur_ref[tb - tq - BAND:tb, lanes], next_ref[:, lanes]], axis=0)
        return cur_ref[i * tq - BAND:(i + 1) * tq + BAND, lanes]

    units = [(i, p) for i in range(n_sub) for p in range(pairs)]

    def stage_a(u):
        i, p = units[u]
        slot = u % 2
        lanes = slice(p * LANES, (p + 1) * LANES)
        q = q_ref[i * tq:(i + 1) * tq, lanes]
        zero = jnp.zeros_like(q)
        q2 = jnp.concatenate([jnp.where(first_head, q, zero), jnp.where(first_head, zero, q)],
                             axis=0)
        kw = window(kp_ref, kc_ref, kn_ref, i, lanes)
        s = lax.dot_general(q2, kw, (((1,), (1,)), ((), ())), preferred_element_type=F32)
        if i == 0:
            idx = jnp.where(j == 0, 1, 0)
        elif i == n_sub - 1:
            idx = jnp.where(j == n_chunks - 1, 2, 0)
        else:
            idx = 0
        s_ref[slot] = s + bias_ref[idx]

    def stage_b(u):
        slot = u % 2
        for r in range(0, tq, row_chunk):
            s0 = s_ref[slot, r:r + row_chunk, :]
            s1 = s_ref[slot, tq + r:tq + r + row_chunk, :]
            m0 = jnp.max(s0, axis=-1, keepdims=True)
            m1 = jnp.max(s1, axis=-1, keepdims=True)
            e0 = jnp.exp2(s0 - m0)
            e1 = jnp.exp2(s1 - m1)
            p_ref[slot, r:r + row_chunk, :] = e0.astype(BF16)
            p_ref[slot, tq + r:tq + r + row_chunk, :] = e1.astype(BF16)
            m_ref[slot, r:r + row_chunk, :] = jnp.where(
                first_chunk, jnp.broadcast_to(m0, (row_chunk, LANES)),
                jnp.broadcast_to(m1, (row_chunk, LANES)))
            d0 = jnp.sum(e0, axis=-1, keepdims=True)
            d1 = jnp.sum(e1, axis=-1, keepdims=True)
            d_ref[slot, r:r + row_chunk, :] = jnp.where(
                first_chunk, jnp.broadcast_to(d0, (row_chunk, LANES)),
                jnp.broadcast_to(d1, (row_chunk, LANES)))

    def stage_c(u):
        i, p = units[u]
        slot = u % 2
        lanes = slice(p * LANES, (p + 1) * LANES)
        pv = jnp.dot(p_ref[slot], window(vp_ref, vc_ref, vn_ref, i, lanes),
                     preferred_element_type=F32)
        o = jnp.where(first_head, pv[:tq], pv[tq:])
        dens = d_ref[slot]
        o_ref[i * tq:(i + 1) * tq, lanes] = (o * (1.0 / dens)).astype(BF16)
        lse = m_ref[slot] * LN2 + jnp.log(dens)
        cur = jnp.zeros((tq, LANES), F32) if p == 0 else lse_ref[i * tq:(i + 1) * tq, :]
        lse_ref[i * tq:(i + 1) * tq, :] = jnp.where(
            jnp.bitwise_and(lane, HEAD_DIM - 1) == p, lse, cur)

    n_units = len(units)
    for t in range(n_units + 2):
        if t < n_units:
            stage_a(t)
        if 0 <= t - 1 < n_units:
            stage_b(t - 1)
        if 0 <= t - 2 < n_units:
            stage_c(t - 2)


def _attn_bias(tq):
    tk = tq + 2 * BAND
    col = np.arange(tk)[None, :]
    row = np.arange(tq)[:, None]
    band = (col - row >= 0) & (col - row <= 2 * BAND)
    variants = [band, band & (col >= BAND), band & (col < tq + BAND)]
    bias = np.stack([np.where(v, 0.0, MASK_VALUE) for v in variants]).astype(np.float32)
    return jnp.asarray(np.concatenate([bias, bias], axis=1))


def _attention(qkv, *, attn_width, tq=128, max_tb=512):
    batch, d, t_len, width = qkv.shape
    tb = min(max_tb, t_len)
    assert t_len % tb == 0 and tb % tq == 0 and tb // tq >= 2
    n_chunks = t_len // tb
    pairs = attn_width // LANES
    hb = tb // BAND
    last_hb = t_len // BAND - 1
    tk = tq + 2 * BAND

    def cur(off):
        return pl.BlockSpec((None, None, tb, attn_width), lambda b, r, j: (b, r, j, off))

    def prv(off):
        return pl.BlockSpec((None, None, BAND, attn_width),
                            lambda b, r, j: (b, r, jnp.maximum(j * hb - 1, 0), off))

    def nxt(off):
        return pl.BlockSpec((None, None, BAND, attn_width),
                            lambda b, r, j: (b, r, jnp.minimum((j + 1) * hb, last_hb), off))

    kern = functools.partial(_attn_kernel, tb=tb, tq=tq, n_chunks=n_chunks, pairs=pairs)
    return pl.pallas_call(
        kern,
        grid=(batch, d, n_chunks),
        in_specs=[pl.BlockSpec((3, 2 * tq, tk), lambda b, r, j: (0, 0, 0)),
                  cur(0), prv(1), cur(1), nxt(1), prv(2), cur(2), nxt(2)],
        out_specs=[
            pl.BlockSpec((None, None, tb, attn_width), lambda b, r, j: (b, r, j, 0)),
            pl.BlockSpec((None, None, tb, LANES), lambda b, r, j: (b, r, j, 0)),
        ],
        out_shape=[
            jax.ShapeDtypeStruct((batch, d, t_len, attn_width), BF16),
            jax.ShapeDtypeStruct((batch, d, t_len, LANES), F32),
        ],
        scratch_shapes=[pltpu.VMEM((2, 2 * tq, tk), F32),
                        pltpu.VMEM((2, 2 * tq, tk), BF16),
                        pltpu.VMEM((2, tq, LANES), F32),
                        pltpu.VMEM((2, tq, LANES), F32)],
        compiler_params=_cparams(("parallel", "parallel", "parallel")),
        name=f"attn_d{d}",
    )(_attn_bias(tq), qkv, qkv, qkv, qkv, qkv, qkv, qkv)


def _fourier_kernel(x_ref, m1_ref, m2_ref, csc_ref, o_ref, xf_ref, yr_ref, yi_ref,
                    *, n1, n2, pitch, scale):
    xf_ref[...] = x_ref[...].astype(F32)

    def stage1(m, carry):
        xm = xf_ref[pl.ds(m, n1, stride=n2), :].astype(BF16)
        y = jnp.dot(m1_ref[m], xm, preferred_element_type=F32)
        yr_ref[pl.ds(m, n1, stride=pitch), :] = y[:n1]
        yi_ref[pl.ds(m, n1, stride=pitch), :] = y[n1:]
        return carry

    lax.fori_loop(0, n2, stage1, 0, unroll=8)

    m2 = m2_ref[...]
    csc = csc_ref[...]

    def stage2(kk, carry):
        ka = 2 * kk
        kb = ka + 1
        base_a = pl.multiple_of(ka * pitch, 8)
        base_b = pl.multiple_of(kb * pitch, 8)
        dat_a = jnp.concatenate([yr_ref[pl.ds(base_a, n2), :], yi_ref[pl.ds(base_a, n2), :]], axis=0)
        dat_b = jnp.concatenate([yr_ref[pl.ds(base_b, n2), :], yi_ref[pl.ds(base_b, n2), :]], axis=0)
        dat = jnp.concatenate([dat_a, dat_b], axis=1).astype(BF16)
        o = jnp.dot(m2, dat, preferred_element_type=F32).astype(BF16)
        lhs = jnp.concatenate(
            [jnp.concatenate([o[:n2, :LANES], o[n2:, :LANES]], axis=1),
             jnp.concatenate([o[:n2, LANES:], o[n2:, LANES:]], axis=1)], axis=0)
        res = jnp.dot(lhs, csc, preferred_element_type=F32) * scale
        o_ref[pl.ds(ka, n2, stride=n1), :] = res[:n2]
        o_ref[pl.ds(kb, n2, stride=n1), :] = res[n2:]
        return carry

    lax.fori_loop(0, n1 // 2, stage2, 0, unroll=4)


def _dft_mats(n):
    k = np.arange(n)
    ang = 2.0 * np.pi * ((k[:, None] * k[None, :]) % n) / n
    return np.cos(ang), np.sin(ang)


def _fourier(f3, *, first_lane_block=0):
    batch, seq, _ = f3.shape
    ch = FOURIER_GROUP
    groups = N_FOURIER_GROUPS
    width = groups * ch
    n1 = DFT_N1
    n2 = seq // n1
    pitch = n2 + DFT_PAD
    pos = n2 * np.arange(n1)[None, None, :] + np.arange(n2)[:, None, None]
    ang = 2.0 * np.pi * ((np.arange(n1)[None, :, None] * pos) % seq) / seq
    m1 = jnp.asarray(np.concatenate([np.cos(ang), -np.sin(ang)], axis=1), F32).astype(BF16)
    c2, s2 = _dft_mats(n2)
    m2 = jnp.asarray(np.block([[c2, s2], [-s2, c2]]), F32).astype(BF16)
    cch, sch = _dft_mats(ch)
    csc = jnp.asarray(np.concatenate([cch, sch], axis=0), F32).astype(BF16)
    assert n1 % 2 == 0
    kern = functools.partial(_fourier_kernel, n1=n1, n2=n2, pitch=pitch,
                             scale=1.0 / math.sqrt(seq * ch))
    return pl.pallas_call(
        kern,
        grid=(batch, groups),
        in_specs=[
            pl.BlockSpec((None, seq, ch), lambda b, g: (b, 0, first_lane_block + g)),
            pl.BlockSpec((n2, 2 * n1, n1), lambda b, g: (0, 0, 0)),
            pl.BlockSpec((2 * n2, 2 * n2), lambda b, g: (0, 0)),
            pl.BlockSpec((2 * ch, ch), lambda b, g: (0, 0)),
        ],
        out_specs=pl.BlockSpec((None, seq, ch), lambda b, g: (b, 0, g)),
        out_shape=jax.ShapeDtypeStruct((batch, seq, width), F32),
        scratch_shapes=[pltpu.VMEM((seq, ch), F32),
                        pltpu.VMEM((n1 * pitch, ch), F32),
                        pltpu.VMEM((n1 * pitch, ch), F32)],
        compiler_params=_cparams(("parallel", "parallel")),
        name="fourier",
    )(f3, m1, m2, csc)


def _out_proj_kernel(o1_ref, o4_ref, o16_ref, l1_ref, l4_ref, l16_ref, four_ref, x_ref,
                     ga_ref, gf_ref, wa_ref, wf_ref, g2_ref, e_ref, x1_ref, h2_ref,
                     n4_ref, n16_ref, ln4_ref, ln16_ref):
    tm = x_ref.shape[0]
    n_slabs = n4_ref.shape[0]

    for r in range(4):
        blk = o4_ref[r].astype(F32)
        for c in range(n_slabs):
            n4_ref[c, pl.ds(r, tm // 4, stride=4), :] = blk[:, c * LANES:(c + 1) * LANES]
        ln4_ref[pl.ds(r, tm // 4, stride=4), :] = l4_ref[r]
    for r in range(16):
        blk = o16_ref[r].astype(F32)
        for c in range(n_slabs):
            n16_ref[c, pl.ds(r, tm // 16, stride=16), :] = blk[:, c * LANES:(c + 1) * LANES]
        ln16_ref[pl.ds(r, tm // 16, stride=16), :] = l16_ref[r]

    l1 = l1_ref[...]
    l2 = ln4_ref[...]
    l3 = ln16_ref[...]
    m = jnp.maximum(jnp.maximum(l1, l2), l3)
    e1 = jnp.exp(l1 - m)
    e2 = jnp.exp(l2 - m)
    e3 = jnp.exp(l3 - m)
    inv = 1.0 / (e1 + e2 + e3)
    expand = e_ref[...]

    def spread(w):
        hi = w.astype(BF16)
        lo = (w - hi.astype(F32)).astype(BF16)
        return jnp.dot(hi, expand, preferred_element_type=F32) \
            + jnp.dot(lo, expand, preferred_element_type=F32)

    w1 = spread(e1 * inv)
    w2 = spread(e2 * inv)
    o1 = o1_ref[...].astype(F32)
    slabs = []
    ssq = jnp.zeros((tm, 1), F32)
    for c in range(n_slabs):
        lanes = slice(c * LANES, (c + 1) * LANES)
        o3 = n16_ref[c]
        a = o3 + w1[:, lanes] * (o1[:, lanes] - o3) + w2[:, lanes] * (n4_ref[c] - o3)
        ssq = ssq + jnp.sum(a * a, axis=-1, keepdims=True)
        slabs.append(a)
    attn = jnp.concatenate(slabs, axis=1)
    width = n_slabs * LANES
    a_n = (attn * lax.rsqrt(ssq / width + EPS) * ga_ref[...]).astype(BF16)
    f_n = _rms(four_ref[...], gf_ref[...]).astype(BF16)
    y = jnp.dot(a_n, wa_ref[...], preferred_element_type=F32) \
        + jnp.dot(f_n, wf_ref[...], preferred_element_type=F32)
    x1 = x_ref[...] + y
    x1_ref[...] = x1
    h2_ref[...] = _rms(x1, g2_ref[...]).astype(BF16)


def _out_proj(o1, o4, o16, l1, l4, l16, four, x2, ga, gf, wa, wf, g2, *, seq, tm=256):
    ntok, d_model = x2.shape
    aw = o1.shape[1]
    fw = four.shape[1]
    n_heads = aw // HEAD_DIM
    tps = seq // tm
    expand = np.zeros((LANES, aw), np.float32)
    for h in range(n_heads):
        expand[(h % 2) * HEAD_DIM + h // 2, h * HEAD_DIM:(h + 1) * HEAD_DIM] = 1.0
    expand = jnp.asarray(expand).astype(BF16)
    row = lambda w: pl.BlockSpec((tm, w), lambda i: (i, 0))
    cls = lambda d, w: pl.BlockSpec((None, d, tm // d, w), lambda i: (i // tps, 0, i % tps, 0))
    const = lambda a, b: pl.BlockSpec((a, b), lambda i: (0, 0))
    return pl.pallas_call(
        _out_proj_kernel,
        grid=(ntok // tm,),
        in_specs=[row(aw), cls(4, aw), cls(16, aw), row(LANES), cls(4, LANES), cls(16, LANES),
                  row(fw), row(d_model),
                  const(1, aw), const(1, fw), const(aw, d_model), const(fw, d_model),
                  const(1, d_model), const(LANES, aw)],
        out_specs=[row(d_model), row(d_model)],
        out_shape=[jax.ShapeDtypeStruct((ntok, d_model), F32),
                   jax.ShapeDtypeStruct((ntok, d_model), BF16)],
        scratch_shapes=[pltpu.VMEM((aw // LANES, tm, LANES), F32),
                        pltpu.VMEM((aw // LANES, tm, LANES), F32),
                        pltpu.VMEM((tm, LANES), F32),
                        pltpu.VMEM((tm, LANES), F32)],
        compiler_params=_cparams(("parallel",)),
        name="out_proj",
    )(o1, o4, o16, l1, l4, l16, four, x2, ga, gf, wa, wf, g2, expand)


HALO = 16


def _up_kernel(hp_ref, hc_ref, hn_ref, wg_ref, wv_ref, cwg_ref, cwv_ref, cbg_ref, cbv_ref,
               o_ref, lhs_ref, ug_ref, uv_ref, *, tm, tiles_per_seq):
    i = pl.program_id(0)
    j = pl.program_id(1)

    @pl.when(j == 0)
    def _():
        t = i % tiles_per_seq
        zero = jnp.zeros((HALO, lhs_ref.shape[1]), BF16)
        lhs_ref[0:HALO] = jnp.where(t == 0, zero, hp_ref[...])
        lhs_ref[HALO:HALO + tm] = hc_ref[...]
        lhs_ref[HALO + tm:HALO + tm + HALO] = jnp.where(t == tiles_per_seq - 1, zero, hn_ref[...])

    lhs = lhs_ref[...]
    ug_ref[...] = jnp.dot(lhs, wg_ref[...], preferred_element_type=F32)
    uv_ref[...] = jnp.dot(lhs, wv_ref[...], preferred_element_type=F32)

    def conv(u_ref, cw_ref, cb_ref):
        cw = cw_ref[...]
        return (u_ref[HALO - 1:HALO - 1 + tm] * cw[0:1]
                + u_ref[HALO:HALO + tm] * cw[1:2]
                + u_ref[HALO + 1:HALO + 1 + tm] * cw[2:3]
                + cb_ref[...])

    gate = conv(ug_ref, cwg_ref, cbg_ref)
    val = conv(uv_ref, cwv_ref, cbv_ref)
    o_ref[...] = (gate * (1.0 / (1.0 + jnp.exp(-gate))) * val).astype(BF16)


def _up_proj(h2, w_up, conv_w, conv_b, *, seq, tm=1024, tn=512):
    ntok, d_model = h2.shape
    d_ff = w_up.shape[1] // 2
    nj = d_ff // tn
    tps = seq // tm
    hb = tm // HALO
    last = ntok // HALO - 1
    kern = functools.partial(_up_kernel, tm=tm, tiles_per_seq=tps)
    return pl.pallas_call(
        kern,
        grid=(ntok // tm, nj),
        in_specs=[
            pl.BlockSpec((HALO, d_model), lambda i, j: (jnp.maximum(i * hb - 1, 0), 0)),
            pl.BlockSpec((tm, d_model), lambda i, j: (i, 0)),
            pl.BlockSpec((HALO, d_model), lambda i, j: (jnp.minimum((i + 1) * hb, last), 0)),
            pl.BlockSpec((d_model, tn), lambda i, j: (0, j)),
            pl.BlockSpec((d_model, tn), lambda i, j: (0, nj + j)),
            pl.BlockSpec((3, tn), lambda i, j: (0, j)),
            pl.BlockSpec((3, tn), lambda i, j: (0, nj + j)),
            pl.BlockSpec((1, tn), lambda i, j: (0, j)),
            pl.BlockSpec((1, tn), lambda i, j: (0, nj + j)),
        ],
        out_specs=pl.BlockSpec((tm, tn), lambda i, j: (i, j)),
        out_shape=jax.ShapeDtypeStruct((ntok, d_ff), BF16),
        scratch_shapes=[pltpu.VMEM((tm + 2 * HALO, d_model), BF16),
                        pltpu.VMEM((tm + 2 * HALO, tn), F32),
                        pltpu.VMEM((tm + 2 * HALO, tn), F32)],
        compiler_params=_cparams(("parallel", "arbitrary")),
        name="up_proj",
    )(h2, h2, h2, w_up, w_up, conv_w, conv_w, conv_b, conv_b)


def _down_kernel(a_ref, w_ref, x1_ref, g_ref, y_ref, *, tn, n_blocks):
    n = pl.program_id(1)
    y = jnp.dot(a_ref[...], w_ref[...], preferred_element_type=F32)
    for b in range(n_blocks):
        @pl.when(n == b)
        def _(b=b):
            cols = slice(b * tn, (b + 1) * tn)
            y_ref[:, cols] = x1_ref[...] + y

    @pl.when(n == n_blocks - 1)
    def _():
        y_ref[...] = _rms(y_ref[...], g_ref[...])


def _down_proj(act, w_down, x1, final_g, *, tm=1024, tn=256):
    ntok, d_ff = act.shape
    d_model = w_down.shape[1]
    n_blocks = d_model // tn
    kern = functools.partial(_down_kernel, tn=tn, n_blocks=n_blocks)
    return pl.pallas_call(
        kern,
        grid=(ntok // tm, n_blocks),
        in_specs=[
            pl.BlockSpec((tm, d_ff), lambda i, n: (i, 0)),
            pl.BlockSpec((d_ff, tn), lambda i, n: (0, n)),
            pl.BlockSpec((tm, tn), lambda i, n: (i, n)),
            pl.BlockSpec((1, d_model), lambda i, n: (0, 0)),
        ],
        out_specs=pl.BlockSpec((tm, d_model), lambda i, n: (i, 0)),
        out_shape=jax.ShapeDtypeStruct((ntok, d_model), F32),
        compiler_params=_cparams(("parallel", "arbitrary")),
        name="down_proj",
    )(act, w_down, x1, final_g)


def _rope_lane_tables(seq):
    inv_freq = ROPE_THETA ** (-jnp.arange(0, ROPE_DIM, 2, dtype=F32) / ROPE_DIM)
    ang = jnp.arange(seq, dtype=F32)[:, None] * inv_freq[None, :]
    cos, sin = jnp.cos(ang), jnp.sin(ang)
    half = ROPE_DIM // 2
    rest = HEAD_DIM - ROPE_DIM
    cos_h = jnp.concatenate([cos, cos, jnp.ones((seq, rest), F32)], axis=1)
    sa_h = jnp.concatenate([-sin, jnp.zeros((seq, HEAD_DIM - half), F32)], axis=1)
    sb_h = jnp.concatenate([jnp.zeros((seq, half), F32), sin, jnp.zeros((seq, rest), F32)], axis=1)
    reps = LANES // HEAD_DIM
    rope = jnp.stack([jnp.tile(t, (1, reps)) for t in (cos_h, sa_h, sb_h)])
    ident = jnp.stack([jnp.ones((seq, LANES), F32), jnp.zeros((seq, LANES), F32),
                       jnp.zeros((seq, LANES), F32)])
    return jnp.stack([rope, ident])


def _encoder(x, p):
    batch, seq, d_model = x.shape
    aw = p["attn_width"]
    ntok = batch * seq
    x2 = x.reshape(ntok, d_model)

    tn = N_FOURIER_GROUPS * FOURIER_GROUP
    nat, c4, c16 = _in_proj(x2, p["norm1_g"], p["w_in"], p["rope"], batch=batch, seq=seq,
                            n_rope_tiles=2 * aw // tn, tn=tn)
    nat3 = nat.reshape(batch, seq, -1)
    o1, l1 = _attention(nat3.reshape(batch, 1, seq, -1), attn_width=aw)
    o4, l4 = _attention(c4, attn_width=aw)
    o16, l16 = _attention(c16, attn_width=aw)
    four = _fourier(nat3, first_lane_block=3 * aw // FOURIER_GROUP)

    x1, h2 = _out_proj(o1.reshape(ntok, aw), o4, o16, l1.reshape(ntok, LANES), l4, l16,
                       four.reshape(ntok, -1), x2,
                       p["attn_out_g"], p["fourier_out_g"], p["w_out_a"], p["w_out_f"],
                       p["norm2_g"], seq=seq)
    act = _up_proj(h2, p["w_up"], p["conv_w"], p["conv_b"], seq=seq)
    y = _down_proj(act, p["w_down"], x1, p["final_g"])
    return y.reshape(batch, seq, d_model)


def _prepare(seq, norm1_g, w_in, attn_out_g, fourier_out_g, w_out, norm2_g, w_up, conv_w, conv_b,
             w_down, final_g):
    aw = (w_in.shape[1] - N_FOURIER_GROUPS * FOURIER_GROUP) // 3
    col_scale = jnp.where(jnp.arange(w_in.shape[1]) < aw, LOG2E / math.sqrt(HEAD_DIM), 1.0)
    w_out_b = w_out.astype(BF16)
    return {
        "attn_width": aw,
        "rope": _rope_lane_tables(seq),
        "norm1_g": norm1_g.reshape(1, -1),
        "w_in": (w_in * col_scale[None, :].astype(F32)).astype(BF16),
        "attn_out_g": attn_out_g.reshape(1, -1),
        "fourier_out_g": fourier_out_g.reshape(1, -1),
        "w_out_a": w_out_b[:aw],
        "w_out_f": w_out_b[aw:],
        "norm2_g": norm2_g.reshape(1, -1),
        "w_up": w_up.astype(BF16),
        "conv_w": conv_w,
        "conv_b": conv_b.reshape(1, -1),
        "w_down": w_down.astype(BF16),
        "final_g": final_g.reshape(1, -1),
    }


def kernel(x_prompt, x_sample, norm1_g, w_in, attn_out_g, fourier_out_g, w_out, norm2_g, w_up,
           conv_w, conv_b, w_down, final_g):
    assert norm1_g.shape[0] == 1, "single-layer encoder"
    assert x_prompt.shape[1] == x_sample.shape[1]
    p = _prepare(x_prompt.shape[1], norm1_g[0], w_in[0], attn_out_g[0], fourier_out_g[0], w_out[0],
                 norm2_g[0], w_up[0], conv_w[0], conv_b[0], w_down[0], final_g)
    return (_encoder(x_prompt, p), _encoder(x_sample, p))
```

```python
import functools
import math

import numpy as np
import jax
import jax.numpy as jnp
from jax import lax
from jax.experimental import pallas as pl
from jax.experimental.pallas import tpu as pltpu

F32 = jnp.float32
BF16 = jnp.bfloat16

HEAD_DIM = 64
ROPE_DIM = 16
ROPE_THETA = 500000.0
N_FOURIER_GROUPS = 4
FOURIER_GROUP = 128
DILATIONS = (1, 4, 16)
BAND = 64
EPS = 1e-6
MASK_VALUE = -1e30
LOG2E = 1.4426950408889634
LN2 = 0.6931471805599453

LANES = 128
VMEM_LIMIT_BYTES = 56 * 1024 * 1024

DFT_N1 = 64
DFT_PAD = 8


def _cparams(semantics):
    return pltpu.CompilerParams(dimension_semantics=semantics,
                                vmem_limit_bytes=VMEM_LIMIT_BYTES)


def _rms(x, g):
    var = jnp.mean(x * x, axis=-1, keepdims=True)
    return x * lax.rsqrt(var + EPS) * g


PERM_ROWS = 256


def _class_permutation():
    p = np.zeros((2 * PERM_ROWS, PERM_ROWS), np.float32)
    for d, base in ((4, 0), (16, PERM_ROWS)):
        per = PERM_ROWS // d
        for r in range(d):
            for i in range(per):
                p[base + r * per + i, d * i + r] = 1.0
    return p


def _in_proj_kernel(x_ref, g_ref, w_ref, tab_ref, perm_ref, nat_ref, c4_ref, c16_ref,
                    h_ref, acc_ref, *, nj):
    n = pl.program_id(0)
    slot = n % 2
    tm = acc_ref.shape[1]
    n_chunks = acc_ref.shape[2] // LANES

    @pl.when(n == 0)
    def _():
        acc_ref[1] = jnp.zeros(acc_ref.shape[1:], F32)

    @pl.when(n % nj == 0)
    def _():
        h_ref[...] = _rms(x_ref[...], g_ref[...]).astype(BF16)

    cos = tab_ref[0]
    sa = tab_ref[1]
    sb = tab_ref[2]
    for c in range(n_chunks):
        lanes = slice(c * LANES, (c + 1) * LANES)
        a = acc_ref[1 - slot, :, lanes]
        r = a * cos + pltpu.roll(a, LANES - ROPE_DIM // 2, 1) * sa \
            + pltpu.roll(a, ROPE_DIM // 2, 1) * sb
        nat_ref[:, lanes] = r.astype(BF16)

    acc_ref[slot] = jnp.dot(h_ref[...], w_ref[...], preferred_element_type=F32)

    perm = perm_ref[...]
    per4 = PERM_ROWS // 4
    per16 = PERM_ROWS // 16
    for s in range(tm // PERM_ROWS):
        slab = nat_ref[s * PERM_ROWS:(s + 1) * PERM_ROWS, :]
        res = jnp.dot(perm, slab, preferred_element_type=F32).astype(BF16)
        for r4 in range(4):
            c4_ref[r4, s * per4:(s + 1) * per4, :] = res[r4 * per4:(r4 + 1) * per4]
        for r16 in range(16):
            c16_ref[r16, s * per16:(s + 1) * per16, :] = \
                res[PERM_ROWS + r16 * per16:PERM_ROWS + (r16 + 1) * per16]


def _in_proj(x2, g, w, tabs, *, batch, seq, n_rope_tiles, tm=1024, tn=512):
    ntok, d_model = x2.shape
    width = w.shape[1]
    nj = width // tn
    ntiles = ntok // tm
    assert tm % PERM_ROWS == 0
    tps = seq // tm
    kern = functools.partial(_in_proj_kernel, nj=nj)
    perm = jnp.asarray(_class_permutation()).astype(BF16)

    def prev(n):
        pn = jnp.maximum(n - 1, 0)
        return pn // nj, pn % nj

    def tab_map(n):
        pi, pj = prev(n)
        return (jnp.where(pj < n_rope_tiles, 0, 1), 0, pi % tps, 0)

    def nat_map(n):
        return prev(n)

    def cls_map(n):
        pi, pj = prev(n)
        return (pi // tps, 0, pi % tps, pj)

    return pl.pallas_call(
        kern,
        grid=(ntiles * nj + 1,),
        in_specs=[
            pl.BlockSpec((tm, d_model), lambda n: (jnp.minimum(n // nj, ntiles - 1), 0)),
            pl.BlockSpec((1, d_model), lambda n: (0, 0)),
            pl.BlockSpec((d_model, tn), lambda n: (0, n % nj)),
            pl.BlockSpec((None, 3, tm, LANES), tab_map),
            pl.BlockSpec((2 * PERM_ROWS, PERM_ROWS), lambda n: (0, 0)),
        ],
        out_specs=[
            pl.BlockSpec((tm, tn), nat_map),
            pl.BlockSpec((None, 4, tm // 4, tn), cls_map),
            pl.BlockSpec((None, 16, tm // 16, tn), cls_map),
        ],
        out_shape=[
            jax.ShapeDtypeStruct((ntok, width), BF16),
            jax.ShapeDtypeStruct((batch, 4, seq // 4, width), BF16),
            jax.ShapeDtypeStruct((batch, 16, seq // 16, width), BF16),
        ],
        scratch_shapes=[pltpu.VMEM((tm, d_model), BF16),
                        pltpu.VMEM((2, tm, tn), F32)],
        compiler_params=_cparams(("arbitrary",)),
        name="in_proj",
    )(x2, g, w, tabs, perm)


def _attn_kernel(bias_ref, q_ref, kp_ref, kc_ref, kn_ref, vp_ref, vc_ref, vn_ref,
                 o_ref, lse_ref, s_ref, p_ref, m_ref, d_ref,
                 *, tb, tq, n_chunks, pairs, row_chunk=16):
    j = pl.program_id(2)
    tk = tq + 2 * BAND
    n_sub = tb // tq

    lane = lax.broadcasted_iota(jnp.int32, (tq, LANES), 1)
    first_head = lane < HEAD_DIM
    first_chunk = lax.broadcasted_iota(jnp.int32, (row_chunk, LANES), 1) < HEAD_DIM

    def window(prev_ref, cur_ref, next_ref, i, lanes):
        if i == 0:
            return jnp.concatenate([prev_ref[:, lanes], cur_ref[0:tq + BAND, lanes]], axis=0)
        if i == n_sub - 1:
            return jnp.concatenate([cur_ref[tb - tq - BAND:tb, lanes], next_ref[:, lanes]], axis=0)
        return cur_ref[i * tq - BAND:(i + 1) * tq + BAND, lanes]

    units = [(i, p) for i in range(n_sub) for p in range(pairs)]

    def stage_a(u):
        i, p = units[u]
        slot = u % 2
        lanes = slice(p * LANES, (p + 1) * LANES)
        q = q_ref[i * tq:(i + 1) * tq, lanes]
        zero = jnp.zeros_like(q)
        q2 = jnp.concatenate([jnp.where(first_head, q, zero), jnp.where(first_head, zero, q)],
                             axis=0)
        kw = window(kp_ref, kc_ref, kn_ref, i, lanes)
        s = lax.dot_general(q2, kw, (((1,), (1,)), ((), ())), preferred_element_type=F32)
        if i == 0:
            idx = jnp.where(j == 0, 1, 0)
        elif i == n_sub - 1:
            idx = jnp.where(j == n_chunks - 1, 2, 0)
        else:
            idx = 0
        s_ref[slot] = s + bias_ref[idx]

    def stage_b(u):
        slot = u % 2
        for r in range(0, tq, row_chunk):
            s0 = s_ref[slot, r:r + row_chunk, :]
            s1 = s_ref[slot, tq + r:tq + r + row_chunk, :]
            m0 = jnp.max(s0, axis=-1, keepdims=True)
            m1 = jnp.max(s1, axis=-1, keepdims=True)
            e0 = jnp.exp2(s0 - m0)
            e1 = jnp.exp2(s1 - m1)
            p_ref[slot, r:r + row_chunk, :] = e0.astype(BF16)
            p_ref[slot, tq + r:tq + r + row_chunk, :] = e1.astype(BF16)
            m_ref[slot, r:r + row_chunk, :] = jnp.where(
                first_chunk, jnp.broadcast_to(m0, (row_chunk, LANES)),
                jnp.broadcast_to(m1, (row_chunk, LANES)))
            d0 = jnp.sum(e0, axis=-1, keepdims=True)
            d1 = jnp.sum(e1, axis=-1, keepdims=True)
            d_ref[slot, r:r + row_chunk, :] = jnp.where(
                first_chunk, jnp.broadcast_to(d0, (row_chunk, LANES)),
                jnp.broadcast_to(d1, (row_chunk, LANES)))

    def stage_c(u):
        i, p = units[u]
        slot = u % 2
        lanes = slice(p * LANES, (p + 1) * LANES)
        pv = jnp.dot(p_ref[slot], window(vp_ref, vc_ref, vn_ref, i, lanes),
                     preferred_element_type=F32)
        o = jnp.where(first_head, pv[:tq], pv[tq:])
        dens = d_ref[slot]
        o_ref[i * tq:(i + 1) * tq, lanes] = (o * (1.0 / dens)).astype(BF16)
        lse = m_ref[slot] * LN2 + jnp.log(dens)
        cur = jnp.zeros((tq, LANES), F32) if p == 0 else lse_ref[i * tq:(i + 1) * tq, :]
        lse_ref[i * tq:(i + 1) * tq, :] = jnp.where(
            jnp.bitwise_and(lane, HEAD_DIM - 1) == p, lse, cur)

    n_units = len(units)
    for t in range(n_units + 2):
        if t < n_units:
            stage_a(t)
        if 0 <= t - 1 < n_units:
            stage_b(t - 1)
        if 0 <= t - 2 < n_units:
            stage_c(t - 2)


def _attn_bias(tq):
    tk = tq + 2 * BAND
    col = np.arange(tk)[None, :]
    row = np.arange(tq)[:, None]
    band = (col - row >= 0) & (col - row <= 2 * BAND)
    variants = [band, band & (col >= BAND), band & (col < tq + BAND)]
    bias = np.stack([np.where(v, 0.0, MASK_VALUE) for v in variants]).astype(np.float32)
    return jnp.asarray(np.concatenate([bias, bias], axis=1))


def _attention(qkv, *, attn_width, tq=128, max_tb=512):
    batch, d, t_len, width = qkv.shape
    tb = min(max_tb, t_len)
    assert t_len % tb == 0 and tb % tq == 0 and tb // tq >= 2
    n_chunks = t_len // tb
    pairs = attn_width // LANES
    hb = tb // BAND
    last_hb = t_len // BAND - 1
    tk = tq + 2 * BAND

    def cur(off):
        return pl.BlockSpec((None, None, tb, attn_width), lambda b, r, j: (b, r, j, off))

    def prv(off):
        return pl.BlockSpec((None, None, BAND, attn_width),
                            lambda b, r, j: (b, r, jnp.maximum(j * hb - 1, 0), off))

    def nxt(off):
        return pl.BlockSpec((None, None, BAND, attn_width),
                            lambda b, r, j: (b, r, jnp.minimum((j + 1) * hb, last_hb), off))

    kern = functools.partial(_attn_kernel, tb=tb, tq=tq, n_chunks=n_chunks, pairs=pairs)
    return pl.pallas_call(
        kern,
        grid=(batch, d, n_chunks),
        in_specs=[pl.BlockSpec((3, 2 * tq, tk), lambda b, r, j: (0, 0, 0)),
                  cur(0), prv(1), cur(1), nxt(1), prv(2), cur(2), nxt(2)],
        out_specs=[
            pl.BlockSpec((None, None, tb, attn_width), lambda b, r, j: (b, r, j, 0)),
            pl.BlockSpec((None, None, tb, LANES), lambda b, r, j: (b, r, j, 0)),
        ],
        out_shape=[
            jax.ShapeDtypeStruct((batch, d, t_len, attn_width), BF16),
            jax.ShapeDtypeStruct((batch, d, t_len, LANES), F32),
        ],
        scratch_shapes=[pltpu.VMEM((2, 2 * tq, tk), F32),
                        pltpu.VMEM((2, 2 * tq, tk), BF16),
                        pltpu.VMEM((2, tq, LANES), F32),
                        pltpu.VMEM((2, tq, LANES), F32)],
        compiler_params=_cparams(("parallel", "parallel", "parallel")),
        name=f"attn_d{d}",
    )(_attn_bias(tq), qkv, qkv, qkv, qkv, qkv, qkv, qkv)


def _fourier_kernel(x_ref, m1_ref, m2_ref, csc_ref, o_ref, xf_ref, yr_ref, yi_ref,
                    *, n1, n2, pitch, scale):
    xf_ref[...] = x_ref[...].astype(F32)

    def stage1(m, carry):
        xm = xf_ref[pl.ds(m, n1, stride=n2), :].astype(BF16)
        y = jnp.dot(m1_ref[m], xm, preferred_element_type=F32)
        yr_ref[pl.ds(m, n1, stride=pitch), :] = y[:n1]
        yi_ref[pl.ds(m, n1, stride=pitch), :] = y[n1:]
        return carry

    lax.fori_loop(0, n2, stage1, 0, unroll=8)

    m2 = m2_ref[...]
    csc = csc_ref[...]

    def stage2(kk, carry):
        ka = 2 * kk
        kb = ka + 1
        base_a = pl.multiple_of(ka * pitch, 8)
        base_b = pl.multiple_of(kb * pitch, 8)
        dat_a = jnp.concatenate([yr_ref[pl.ds(base_a, n2), :], yi_ref[pl.ds(base_a, n2), :]], axis=0)
        dat_b = jnp.concatenate([yr_ref[pl.ds(base_b, n2), :], yi_ref[pl.ds(base_b, n2), :]], axis=0)
        dat = jnp.concatenate([dat_a, dat_b], axis=1).astype(BF16)
        o = jnp.dot(m2, dat, preferred_element_type=F32).astype(BF16)
        lhs = jnp.concatenate(
            [jnp.concatenate([o[:n2, :LANES], o[n2:, :LANES]], axis=1),
             jnp.concatenate([o[:n2, LANES:], o[n2:, LANES:]], axis=1)], axis=0)
        res = jnp.dot(lhs, csc, preferred_element_type=F32) * scale
        o_ref[pl.ds(ka, n2, stride=n1), :] = res[:n2]
        o_ref[pl.ds(kb, n2, stride=n1), :] = res[n2:]
        return carry

    lax.fori_loop(0, n1 // 2, stage2, 0, unroll=4)


def _dft_mats(n):
    k = np.arange(n)
    ang = 2.0 * np.pi * ((k[:, None] * k[None, :]) % n) / n
    return np.cos(ang), np.sin(ang)


def _fourier(f3, *, first_lane_block=0):
    batch, seq, _ = f3.shape
    ch = FOURIER_GROUP
    groups = N_FOURIER_GROUPS
    width = groups * ch
    n1 = DFT_N1
    n2 = seq // n1
    pitch = n2 + DFT_PAD
    pos = n2 * np.arange(n1)[None, None, :] + np.arange(n2)[:, None, None]
    ang = 2.0 * np.pi * ((np.arange(n1)[None, :, None] * pos) % seq) / seq
    m1 = jnp.asarray(np.concatenate([np.cos(ang), -np.sin(ang)], axis=1), F32).astype(BF16)
    c2, s2 = _dft_mats(n2)
    m2 = jnp.asarray(np.block([[c2, s2], [-s2, c2]]), F32).astype(BF16)
    cch, sch = _dft_mats(ch)
    csc = jnp.asarray(np.concatenate([cch, sch], axis=0), F32).astype(BF16)
    assert n1 % 2 == 0
    kern = functools.partial(_fourier_kernel, n1=n1, n2=n2, pitch=pitch,
                             scale=1.0 / math.sqrt(seq * ch))
    return pl.pallas_call(
        kern,
        grid=(batch, groups),
        in_specs=[
            pl.BlockSpec((None, seq, ch), lambda b, g: (b, 0, first_lane_block + g)),
            pl.BlockSpec((n2, 2 * n1, n1), lambda b, g: (0, 0, 0)),
            pl.BlockSpec((2 * n2, 2 * n2), lambda b, g: (0, 0)),
            pl.BlockSpec((2 * ch, ch), lambda b, g: (0, 0)),
        ],
        out_specs=pl.BlockSpec((None, seq, ch), lambda b, g: (b, 0, g)),
        out_shape=jax.ShapeDtypeStruct((batch, seq, width), F32),
        scratch_shapes=[pltpu.VMEM((seq, ch), F32),
                        pltpu.VMEM((n1 * pitch, ch), F32),
                        pltpu.VMEM((n1 * pitch, ch), F32)],
        compiler_params=_cparams(("parallel", "parallel")),
        name="fourier",
    )(f3, m1, m2, csc)


def _out_proj_kernel(o1_ref, o4_ref, o16_ref, l1_ref, l4_ref, l16_ref, four_ref, x_ref,
                     ga_ref, gf_ref, wa_ref, wf_ref, g2_ref, e_ref, x1_ref, h2_ref,
                     n4_ref, n16_ref, ln4_ref, ln16_ref):
    tm = x_ref.shape[0]
    n_slabs = n4_ref.shape[0]

    for r in range(4):
        blk = o4_ref[r].astype(F32)
        for c in range(n_slabs):
            n4_ref[c, pl.ds(r, tm // 4, stride=4), :] = blk[:, c * LANES:(c + 1) * LANES]
        ln4_ref[pl.ds(r, tm // 4, stride=4), :] = l4_ref[r]
    for r in range(16):
        blk = o16_ref[r].astype(F32)
        for c in range(n_slabs):
            n16_ref[c, pl.ds(r, tm // 16, stride=16), :] = blk[:, c * LANES:(c + 1) * LANES]
        ln16_ref[pl.ds(r, tm // 16, stride=16), :] = l16_ref[r]

    l1 = l1_ref[...]
    l2 = ln4_ref[...]
    l3 = ln16_ref[...]
    m = jnp.maximum(jnp.maximum(l1, l2), l3)
    e1 = jnp.exp(l1 - m)
    e2 = jnp.exp(l2 - m)
    e3 = jnp.exp(l3 - m)
    inv = 1.0 / (e1 + e2 + e3)
    expand = e_ref[...]

    def spread(w):
        hi = w.astype(BF16)
        lo = (w - hi.astype(F32)).astype(BF16)
        return jnp.dot(hi, expand, preferred_element_type=F32) \
            + jnp.dot(lo, expand, preferred_element_type=F32)

    w1 = spread(e1 * inv)
    w2 = spread(e2 * inv)
    o1 = o1_ref[...].astype(F32)
    slabs = []
    ssq = jnp.zeros((tm, 1), F32)
    for c in range(n_slabs):
        lanes = slice(c * LANES, (c + 1) * LANES)
        o3 = n16_ref[c]
        a = o3 + w1[:, lanes] * (o1[:, lanes] - o3) + w2[:, lanes] * (n4_ref[c] - o3)
        ssq = ssq + jnp.sum(a * a, axis=-1, keepdims=True)
        slabs.append(a)
    attn = jnp.concatenate(slabs, axis=1)
    width = n_slabs * LANES
    a_n = (attn * lax.rsqrt(ssq / width + EPS) * ga_ref[...]).astype(BF16)
    f_n = _rms(four_ref[...], gf_ref[...]).astype(BF16)
    y = jnp.dot(a_n, wa_ref[...], preferred_element_type=F32) \
        + jnp.dot(f_n, wf_ref[...], preferred_element_type=F32)
    x1 = x_ref[...] + y
    x1_ref[...] = x1
    h2_ref[...] = _rms(x1, g2_ref[...]).astype(BF16)


def _out_proj(o1, o4, o16, l1, l4, l16, four, x2, ga, gf, wa, wf, g2, *, seq, tm=256):
    ntok, d_model = x2.shape
    aw = o1.shape[1]
    fw = four.shape[1]
    n_heads = aw // HEAD_DIM
    tps = seq // tm
    expand = np.zeros((LANES, aw), np.float32)
    for h in range(n_heads):
        expand[(h % 2) * HEAD_DIM + h // 2, h * HEAD_DIM:(h + 1) * HEAD_DIM] = 1.0
    expand = jnp.asarray(expand).astype(BF16)
    row = lambda w: pl.BlockSpec((tm, w), lambda i: (i, 0))
    cls = lambda d, w: pl.BlockSpec((None, d, tm // d, w), lambda i: (i // tps, 0, i % tps, 0))
    const = lambda a, b: pl.BlockSpec((a, b), lambda i: (0, 0))
    return pl.pallas_call(
        _out_proj_kernel,
        grid=(ntok // tm,),
        in_specs=[row(aw), cls(4, aw), cls(16, aw), row(LANES), cls(4, LANES), cls(16, LANES),
                  row(fw), row(d_model),
                  const(1, aw), const(1, fw), const(aw, d_model), const(fw, d_model),
                  const(1, d_model), const(LANES, aw)],
        out_specs=[row(d_model), row(d_model)],
        out_shape=[jax.ShapeDtypeStruct((ntok, d_model), F32),
                   jax.ShapeDtypeStruct((ntok, d_model), BF16)],
        scratch_shapes=[pltpu.VMEM((aw // LANES, tm, LANES), F32),
                        pltpu.VMEM((aw // LANES, tm, LANES), F32),
                        pltpu.VMEM((tm, LANES), F32),
                        pltpu.VMEM((tm, LANES), F32)],
        compiler_params=_cparams(("parallel",)),
        name="out_proj",
    )(o1, o4, o16, l1, l4, l16, four, x2, ga, gf, wa, wf, g2, expand)


HALO = 16


def _up_kernel(hp_ref, hc_ref, hn_ref, wg_ref, wv_ref, cwg_ref, cwv_ref, cbg_ref, cbv_ref,
               o_ref, lhs_ref, ug_ref, uv_ref, *, tm, tiles_per_seq):
    i = pl.program_id(0)
    j = pl.program_id(1)

    @pl.when(j == 0)
    def _():
        t = i % tiles_per_seq
        zero = jnp.zeros((HALO, lhs_ref.shape[1]), BF16)
        lhs_ref[0:HALO] = jnp.where(t == 0, zero, hp_ref[...])
        lhs_ref[HALO:HALO + tm] = hc_ref[...]
        lhs_ref[HALO + tm:HALO + tm + HALO] = jnp.where(t == tiles_per_seq - 1, zero, hn_ref[...])

    lhs = lhs_ref[...]
    ug_ref[...] = jnp.dot(lhs, wg_ref[...], preferred_element_type=F32)
    uv_ref[...] = jnp.dot(lhs, wv_ref[...], preferred_element_type=F32)

    def conv(u_ref, cw_ref, cb_ref):
        cw = cw_ref[...]
        return (u_ref[HALO - 1:HALO - 1 + tm] * cw[0:1]
                + u_ref[HALO:HALO + tm] * cw[1:2]
                + u_ref[HALO + 1:HALO + 1 + tm] * cw[2:3]
                + cb_ref[...])

    gate = conv(ug_ref, cwg_ref, cbg_ref)
    val = conv(uv_ref, cwv_ref, cbv_ref)
    o_ref[...] = (gate * (1.0 / (1.0 + jnp.exp(-gate))) * val).astype(BF16)


def _up_proj(h2, w_up, conv_w, conv_b, *, seq, tm=1024, tn=512):
    ntok, d_model = h2.shape
    d_ff = w_up.shape[1] // 2
    nj = d_ff // tn
    tps = seq // tm
    hb = tm // HALO
    last = ntok // HALO - 1
    kern = functools.partial(_up_kernel, tm=tm, tiles_per_seq=tps)
    return pl.pallas_call(
        kern,
        grid=(ntok // tm, nj),
        in_specs=[
            pl.BlockSpec((HALO, d_model), lambda i, j: (jnp.maximum(i * hb - 1, 0), 0)),
            pl.BlockSpec((tm, d_model), lambda i, j: (i, 0)),
            pl.BlockSpec((HALO, d_model), lambda i, j: (jnp.minimum((i + 1) * hb, last), 0)),
            pl.BlockSpec((d_model, tn), lambda i, j: (0, j)),
            pl.BlockSpec((d_model, tn), lambda i, j: (0, nj + j)),
            pl.BlockSpec((3, tn), lambda i, j: (0, j)),
            pl.BlockSpec((3, tn), lambda i, j: (0, nj + j)),
            pl.BlockSpec((1, tn), lambda i, j: (0, j)),
            pl.BlockSpec((1, tn), lambda i, j: (0, nj + j)),
        ],
        out_specs=pl.BlockSpec((tm, tn), lambda i, j: (i, j)),
        out_shape=jax.ShapeDtypeStruct((ntok, d_ff), BF16),
        scratch_shapes=[pltpu.VMEM((tm + 2 * HALO, d_model), BF16),
                        pltpu.VMEM((tm + 2 * HALO, tn), F32),
                        pltpu.VMEM((tm + 2 * HALO, tn), F32)],
        compiler_params=_cparams(("parallel", "arbitrary")),
        name="up_proj",
    )(h2, h2, h2, w_up, w_up, conv_w, conv_w, conv_b, conv_b)


def _down_kernel(a_ref, w_ref, x1_ref, g_ref, y_ref, *, tn, n_blocks):
    n = pl.program_id(1)
    y = jnp.dot(a_ref[...], w_ref[...], preferred_element_type=F32)
    for b in range(n_blocks):
        @pl.when(n == b)
        def _(b=b):
            cols = slice(b * tn, (b + 1) * tn)
            y_ref[:, cols] = x1_ref[...] + y

    @pl.when(n == n_blocks - 1)
    def _():
        y_ref[...] = _rms(y_ref[...], g_ref[...])


def _down_proj(act, w_down, x1, final_g, *, tm=1024, tn=256):
    ntok, d_ff = act.shape
    d_model = w_down.shape[1]
    n_blocks = d_model // tn
    kern = functools.partial(_down_kernel, tn=tn, n_blocks=n_blocks)
    return pl.pallas_call(
        kern,
        grid=(ntok // tm, n_blocks),
        in_specs=[
            pl.BlockSpec((tm, d_ff), lambda i, n: (i, 0)),
            pl.BlockSpec((d_ff, tn), lambda i, n: (0, n)),
            pl.BlockSpec((tm, tn), lambda i, n: (i, n)),
            pl.BlockSpec((1, d_model), lambda i, n: (0, 0)),
        ],
        out_specs=pl.BlockSpec((tm, d_model), lambda i, n: (i, 0)),
        out_shape=jax.ShapeDtypeStruct((ntok, d_model), F32),
        compiler_params=_cparams(("parallel", "arbitrary")),
        name="down_proj",
    )(act, w_down, x1, final_g)


def _rope_lane_tables(seq):
    inv_freq = ROPE_THETA ** (-jnp.arange(0, ROPE_DIM, 2, dtype=F32) / ROPE_DIM)
    ang = jnp.arange(seq, dtype=F32)[:, None] * inv_freq[None, :]
    cos, sin = jnp.cos(ang), jnp.sin(ang)
    half = ROPE_DIM // 2
    rest = HEAD_DIM - ROPE_DIM
    cos_h = jnp.concatenate([cos, cos, jnp.ones((seq, rest), F32)], axis=1)
    sa_h = jnp.concatenate([-sin, jnp.zeros((seq, HEAD_DIM - half), F32)], axis=1)
    sb_h = jnp.concatenate([jnp.zeros((seq, half), F32), sin, jnp.zeros((seq, rest), F32)], axis=1)
    reps = LANES // HEAD_DIM
    rope = jnp.stack([jnp.tile(t, (1, reps)) for t in (cos_h, sa_h, sb_h)])
    ident = jnp.stack([jnp.ones((seq, LANES), F32), jnp.zeros((seq, LANES), F32),
                       jnp.zeros((seq, LANES), F32)])
    return jnp.stack([rope, ident])


def _encoder(x, p):
    batch, seq, d_model = x.shape
    aw = p["attn_width"]
    ntok = batch * seq
    x2 = x.reshape(ntok, d_model)

    tn = N_FOURIER_GROUPS * FOURIER_GROUP
    nat, c4, c16 = _in_proj(x2, p["norm1_g"], p["w_in"], p["rope"], batch=batch, seq=seq,
                            n_rope_tiles=2 * aw // tn, tn=tn)
    nat3 = nat.reshape(batch, seq, -1)
    o1, l1 = _attention(nat3.reshape(batch, 1, seq, -1), attn_width=aw)
    o4, l4 = _attention(c4, attn_width=aw)
    o16, l16 = _attention(c16, attn_width=aw)
    four = _fourier(nat3, first_lane_block=3 * aw // FOURIER_GROUP)

    x1, h2 = _out_proj(o1.reshape(ntok, aw), o4, o16, l1.reshape(ntok, LANES), l4, l16,
                       four.reshape(ntok, -1), x2,
                       p["attn_out_g"], p["fourier_out_g"], p["w_out_a"], p["w_out_f"],
                       p["norm2_g"], seq=seq)
    act = _up_proj(h2, p["w_up"], p["conv_w"], p["conv_b"], seq=seq)
    y = _down_proj(act, p["w_down"], x1, p["final_g"])
    return y.reshape(batch, seq, d_model)


def _prepare(seq, norm1_g, w_in, attn_out_g, fourier_out_g, w_out, norm2_g, w_up, conv_w, conv_b,
             w_down, final_g):
    aw = (w_in.shape[1] - N_FOURIER_GROUPS * FOURIER_GROUP) // 3
    col_scale = jnp.where(jnp.arange(w_in.shape[1]) < aw, LOG2E / math.sqrt(HEAD_DIM), 1.0)
    w_out_b = w_out.astype(BF16)
    return {
        "attn_width": aw,
        "rope": _rope_lane_tables(seq),
        "norm1_g": norm1_g.reshape(1, -1),
        "w_in": (w_in * col_scale[None, :].astype(F32)).astype(BF16),
        "attn_out_g": attn_out_g.reshape(1, -1),
        "fourier_out_g": fourier_out_g.reshape(1, -1),
        "w_out_a": w_out_b[:aw],
        "w_out_f": w_out_b[aw:],
        "norm2_g": norm2_g.reshape(1, -1),
        "w_up": w_up.astype(BF16),
        "conv_w": conv_w,
        "conv_b": conv_b.reshape(1, -1),
        "w_down": w_down.astype(BF16),
        "final_g": final_g.reshape(1, -1),
    }


def kernel(x_prompt, x_sample, norm1_g, w_in, attn_out_g, fourier_out_g, w_out, norm2_g, w_up,
           conv_w, conv_b, w_down, final_g):
    assert norm1_g.shape[0] == 1, "single-layer encoder"
    assert x_prompt.shape[1] == x_sample.shape[1]
    p = _prepare(x_prompt.shape[1], norm1_g[0], w_in[0], attn_out_g[0], fourier_out_g[0], w_out[0],
                 norm2_g[0], w_up[0], conv_w[0], conv_b[0], w_down[0], final_g)
    return (_encoder(x_prompt, p), _encoder(x_sample, p))
```

```python
import functools
import math

import numpy as np
import jax
import jax.numpy as jnp
from jax import lax
from jax.experimental import pallas as pl
from jax.experimental.pallas import tpu as pltpu

F32 = jnp.float32
BF16 = jnp.bfloat16

HEAD_DIM = 64
ROPE_DIM = 16
ROPE_THETA = 500000.0
N_FOURIER_GROUPS = 4
FOURIER_GROUP = 128
DILATIONS = (1, 4, 16)
BAND = 64
EPS = 1e-6
MASK_VALUE = -1e30
LOG2E = 1.4426950408889634
LN2 = 0.6931471805599453

LANES = 128
VMEM_LIMIT_BYTES = 56 * 1024 * 1024

DFT_N1 = 64
DFT_PAD = 8


def _cparams(semantics):
    return pltpu.CompilerParams(dimension_semantics=semantics,
                                vmem_limit_bytes=VMEM_LIMIT_BYTES)


def _rms(x, g):
    var = jnp.mean(x * x, axis=-1, keepdims=True)
    return x * lax.rsqrt(var + EPS) * g


PERM_ROWS = 256


def _class_permutation():
    p = np.zeros((2 * PERM_ROWS, PERM_ROWS), np.float32)
    for d, base in ((4, 0), (16, PERM_ROWS)):
        per = PERM_ROWS // d
        for r in range(d):
            for i in range(per):
                p[base + r * per + i, d * i + r] = 1.0
    return p


def _in_proj_kernel(x_ref, g_ref, w_ref, tab_ref, perm_ref, nat_ref, c4_ref, c16_ref,
                    h_ref, acc_ref, *, nj):
    n = pl.program_id(0)
    slot = n % 2
    tm = acc_ref.shape[1]
    n_chunks = acc_ref.shape[2] // LANES

    @pl.when(n == 0)
    def _():
        acc_ref[1] = jnp.zeros(acc_ref.shape[1:], F32)

    @pl.when(n % nj == 0)
    def _():
        h_ref[...] = _rms(x_ref[...], g_ref[...]).astype(BF16)

    cos = tab_ref[0]
    sa = tab_ref[1]
    sb = tab_ref[2]
    for c in range(n_chunks):
        lanes = slice(c * LANES, (c + 1) * LANES)
        a = acc_ref[1 - slot, :, lanes]
        r = a * cos + pltpu.roll(a, LANES - ROPE_DIM // 2, 1) * sa \
            + pltpu.roll(a, ROPE_DIM // 2, 1) * sb
        nat_ref[:, lanes] = r.astype(BF16)

    acc_ref[slot] = jnp.dot(h_ref[...], w_ref[...], preferred_element_type=F32)

    perm = perm_ref[...]
    per4 = PERM_ROWS // 4
    per16 = PERM_ROWS // 16
    for s in range(tm // PERM_ROWS):
        slab = nat_ref[s * PERM_ROWS:(s + 1) * PERM_ROWS, :]
        res = jnp.dot(perm, slab, preferred_element_type=F32).astype(BF16)
        for r4 in range(4):
            c4_ref[r4, s * per4:(s + 1) * per4, :] = res[r4 * per4:(r4 + 1) * per4]
        for r16 in range(16):
            c16_ref[r16, s * per16:(s + 1) * per16, :] = \
                res[PERM_ROWS + r16 * per16:PERM_ROWS + (r16 + 1) * per16]


def _in_proj(x2, g, w, tabs, *, batch, seq, n_rope_tiles, tm=1024, tn=512):
    ntok, d_model = x2.shape
    width = w.shape[1]
    nj = width // tn
    ntiles = ntok // tm
    assert tm % PERM_ROWS == 0
    tps = seq // tm
    kern = functools.partial(_in_proj_kernel, nj=nj)
    perm = jnp.asarray(_class_permutation()).astype(BF16)

    def prev(n):
        pn = jnp.maximum(n - 1, 0)
        return pn // nj, pn % nj

    def tab_map(n):
        pi, pj = prev(n)
        return (jnp.where(pj < n_rope_tiles, 0, 1), 0, pi % tps, 0)

    def nat_map(n):
        return prev(n)

    def cls_map(n):
        pi, pj = prev(n)
        return (pi // tps, 0, pi % tps, pj)

    return pl.pallas_call(
        kern,
        grid=(ntiles * nj + 1,),
        in_specs=[
            pl.BlockSpec((tm, d_model), lambda n: (jnp.minimum(n // nj, ntiles - 1), 0)),
            pl.BlockSpec((1, d_model), lambda n: (0, 0)),
            pl.BlockSpec((d_model, tn), lambda n: (0, n % nj)),
            pl.BlockSpec((None, 3, tm, LANES), tab_map),
            pl.BlockSpec((2 * PERM_ROWS, PERM_ROWS), lambda n: (0, 0)),
        ],
        out_specs=[
            pl.BlockSpec((tm, tn), nat_map),
            pl.BlockSpec((None, 4, tm // 4, tn), cls_map),
            pl.BlockSpec((None, 16, tm // 16, tn), cls_map),
        ],
        out_shape=[
            jax.ShapeDtypeStruct((ntok, width), BF16),
            jax.ShapeDtypeStruct((batch, 4, seq // 4, width), BF16),
            jax.ShapeDtypeStruct((batch, 16, seq // 16, width), BF16),
        ],
        scratch_shapes=[pltpu.VMEM((tm, d_model), BF16),
                        pltpu.VMEM((2, tm, tn), F32)],
        compiler_params=_cparams(("arbitrary",)),
        name="in_proj",
    )(x2, g, w, tabs, perm)


def _attn_kernel(bias_ref, q_ref, kp_ref, kc_ref, kn_ref, vp_ref, vc_ref, vn_ref,
                 o_ref, lse_ref, s_ref, p_ref, m_ref, d_ref,
                 *, tb, tq, n_chunks, pairs, row_chunk=16):
    j = pl.program_id(2)
    tk = tq + 2 * BAND
    n_sub = tb // tq

    lane = lax.broadcasted_iota(jnp.int32, (tq, LANES), 1)
    first_head = lane < HEAD_DIM
    first_chunk = lax.broadcasted_iota(jnp.int32, (row_chunk, LANES), 1) < HEAD_DIM

    def window(prev_ref, cur_ref, next_ref, i, lanes):
        if i == 0:
            return jnp.concatenate([prev_ref[:, lanes], cur_ref[0:tq + BAND, lanes]], axis=0)
        if i == n_sub - 1:
            return jnp.concatenate([cur_ref[tb - tq - BAND:tb, lanes], next_ref[:, lanes]], axis=0)
        return cur_ref[i * tq - BAND:(i + 1) * tq + BAND, lanes]

    units = [(i, p) for i in range(n_sub) for p in range(pairs)]

    def stage_a(u):
        i, p = units[u]
        slot = u % 2
        lanes = slice(p * LANES, (p + 1) * LANES)
        q = q_ref[i * tq:(i + 1) * tq, lanes]
        zero = jnp.zeros_like(q)
        q2 = jnp.concatenate([jnp.where(first_head, q, zero), jnp.where(first_head, zero, q)],
                             axis=0)
        kw = window(kp_ref, kc_ref, kn_ref, i, lanes)
        s = lax.dot_general(q2, kw, (((1,), (1,)), ((), ())), preferred_element_type=F32)
        if i == 0:
            idx = jnp.where(j == 0, 1, 0)
        elif i == n_sub - 1:
            idx = jnp.where(j == n_chunks - 1, 2, 0)
        else:
            idx = 0
        s_ref[slot] = s + bias_ref[idx]

    def stage_b(u):
        slot = u % 2
        for r in range(0, tq, row_chunk):
            s0 = s_ref[slot, r:r + row_chunk, :]
            s1 = s_ref[slot, tq + r:tq + r + row_chunk, :]
            m0 = jnp.max(s0, axis=-1, keepdims=True)
            m1 = jnp.max(s1, axis=-1, keepdims=True)
            e0 = jnp.exp2(s0 - m0)
            e1 = jnp.exp2(s1 - m1)
            p_ref[slot, r:r + row_chunk, :] = e0.astype(BF16)
            p_ref[slot, tq + r:tq + r + row_chunk, :] = e1.astype(BF16)
            m_ref[slot, r:r + row_chunk, :] = jnp.where(
                first_chunk, jnp.broadcast_to(m0, (row_chunk, LANES)),
                jnp.broadcast_to(m1, (row_chunk, LANES)))
            d0 = jnp.sum(e0, axis=-1, keepdims=True)
            d1 = jnp.sum(e1, axis=-1, keepdims=True)
            d_ref[slot, r:r + row_chunk, :] = jnp.where(
                first_chunk, jnp.broadcast_to(d0, (row_chunk, LANES)),
                jnp.broadcast_to(d1, (row_chunk, LANES)))

    def stage_c(u):
        i, p = units[u]
        slot = u % 2
        lanes = slice(p * LANES, (p + 1) * LANES)
        pv = jnp.dot(p_ref[slot], window(vp_ref, vc_ref, vn_ref, i, lanes),
                     preferred_element_type=F32)
        o = jnp.where(first_head, pv[:tq], pv[tq:])
        dens = d_ref[slot]
        o_ref[i * tq:(i + 1) * tq, lanes] = (o * (1.0 / dens)).astype(BF16)
        lse = m_ref[slot] * LN2 + jnp.log(dens)
        cur = jnp.zeros((tq, LANES), F32) if p == 0 else lse_ref[i * tq:(i + 1) * tq, :]
        lse_ref[i * tq:(i + 1) * tq, :] = jnp.where(
            jnp.bitwise_and(lane, HEAD_DIM - 1) == p, lse, cur)

    n_units = len(units)
    for t in range(n_units + 2):
        if t < n_units:
            stage_a(t)
        if 0 <= t - 1 < n_units:
            stage_b(t - 1)
        if 0 <= t - 2 < n_units:
            stage_c(t - 2)


def _attn_bias(tq):
    tk = tq + 2 * BAND
    col = np.arange(tk)[None, :]
    row = np.arange(tq)[:, None]
    band = (col - row >= 0) & (col - row <= 2 * BAND)
    variants = [band, band & (col >= BAND), band & (col < tq + BAND)]
    bias = np.stack([np.where(v, 0.0, MASK_VALUE) for v in variants]).astype(np.float32)
    return jnp.asarray(np.concatenate([bias, bias], axis=1))


def _attention(qkv, *, attn_width, tq=128, max_tb=512):
    batch, d, t_len, width = qkv.shape
    tb = min(max_tb, t_len)
    assert t_len % tb == 0 and tb % tq == 0 and tb // tq >= 2
    n_chunks = t_len // tb
    pairs = attn_width // LANES
    hb = tb // BAND
    last_hb = t_len // BAND - 1
    tk = tq + 2 * BAND

    def cur(off):
        return pl.BlockSpec((None, None, tb, attn_width), lambda b, r, j: (b, r, j, off))

    def prv(off):
        return pl.BlockSpec((None, None, BAND, attn_width),
                            lambda b, r, j: (b, r, jnp.maximum(j * hb - 1, 0), off))

    def nxt(off):
        return pl.BlockSpec((None, None, BAND, attn_width),
                            lambda b, r, j: (b, r, jnp.minimum((j + 1) * hb, last_hb), off))

    kern = functools.partial(_attn_kernel, tb=tb, tq=tq, n_chunks=n_chunks, pairs=pairs)
    return pl.pallas_call(
        kern,
        grid=(batch, d, n_chunks),
        in_specs=[pl.BlockSpec((3, 2 * tq, tk), lambda b, r, j: (0, 0, 0)),
                  cur(0), prv(1), cur(1), nxt(1), prv(2), cur(2), nxt(2)],
        out_specs=[
            pl.BlockSpec((None, None, tb, attn_width), lambda b, r, j: (b, r, j, 0)),
            pl.BlockSpec((None, None, tb, LANES), lambda b, r, j: (b, r, j, 0)),
        ],
        out_shape=[
            jax.ShapeDtypeStruct((batch, d, t_len, attn_width), BF16),
            jax.ShapeDtypeStruct((batch, d, t_len, LANES), F32),
        ],
        scratch_shapes=[pltpu.VMEM((2, 2 * tq, tk), F32),
                        pltpu.VMEM((2, 2 * tq, tk), BF16),
                        pltpu.VMEM((2, tq, LANES), F32),
                        pltpu.VMEM((2, tq, LANES), F32)],
        compiler_params=_cparams(("parallel", "parallel", "parallel")),
        name=f"attn_d{d}",
    )(_attn_bias(tq), qkv, qkv, qkv, qkv, qkv, qkv, qkv)


def _fourier_kernel(x_ref, m1_ref, m2_ref, csc_ref, o_ref, xf_ref, yr_ref, yi_ref,
                    *, n1, n2, pitch, scale):
    xf_ref[...] = x_ref[...].astype(F32)

    def stage1(m, carry):
        xm = xf_ref[pl.ds(m, n1, stride=n2), :].astype(BF16)
        y = jnp.dot(m1_ref[m], xm, preferred_element_type=F32)
        yr_ref[pl.ds(m, n1, stride=pitch), :] = y[:n1]
        yi_ref[pl.ds(m, n1, stride=pitch), :] = y[n1:]
        return carry

    lax.fori_loop(0, n2, stage1, 0, unroll=8)

    m2 = m2_ref[...]
    csc = csc_ref[...]

    def stage2(kk, carry):
        ka = 2 * kk
        kb = ka + 1
        base_a = pl.multiple_of(ka * pitch, 8)
        base_b = pl.multiple_of(kb * pitch, 8)
        dat_a = jnp.concatenate([yr_ref[pl.ds(base_a, n2), :], yi_ref[pl.ds(base_a, n2), :]], axis=0)
        dat_b = jnp.concatenate([yr_ref[pl.ds(base_b, n2), :], yi_ref[pl.ds(base_b, n2), :]], axis=0)
        dat = jnp.concatenate([dat_a, dat_b], axis=1).astype(BF16)
        o = jnp.dot(m2, dat, preferred_element_type=F32).astype(BF16)
        lhs = jnp.concatenate(
            [jnp.concatenate([o[:n2, :LANES], o[n2:, :LANES]], axis=1),
             jnp.concatenate([o[:n2, LANES:], o[n2:, LANES:]], axis=1)], axis=0)
        res = jnp.dot(lhs, csc, preferred_element_type=F32) * scale
        o_ref[pl.ds(ka, n2, stride=n1), :] = res[:n2]
        o_ref[pl.ds(kb, n2, stride=n1), :] = res[n2:]
        return carry

    lax.fori_loop(0, n1 // 2, stage2, 0, unroll=4)


def _dft_mats(n):
    k = np.arange(n)
    ang = 2.0 * np.pi * ((k[:, None] * k[None, :]) % n) / n
    return np.cos(ang), np.sin(ang)


def _fourier(f3, *, first_lane_block=0):
    batch, seq, _ = f3.shape
    ch = FOURIER_GROUP
    groups = N_FOURIER_GROUPS
    width = groups * ch
    n1 = DFT_N1
    n2 = seq // n1
    pitch = n2 + DFT_PAD
    pos = n2 * np.arange(n1)[None, None, :] + np.arange(n2)[:, None, None]
    ang = 2.0 * np.pi * ((np.arange(n1)[None, :, None] * pos) % seq) / seq
    m1 = jnp.asarray(np.concatenate([np.cos(ang), -np.sin(ang)], axis=1), F32).astype(BF16)
    c2, s2 = _dft_mats(n2)
    m2 = jnp.asarray(np.block([[c2, s2], [-s2, c2]]), F32).astype(BF16)
    cch, sch = _dft_mats(ch)
    csc = jnp.asarray(np.concatenate([cch, sch], axis=0), F32).astype(BF16)
    assert n1 % 2 == 0
    kern = functools.partial(_fourier_kernel, n1=n1, n2=n2, pitch=pitch,
                             scale=1.0 / math.sqrt(seq * ch))
    return pl.pallas_call(
        kern,
        grid=(batch, groups),
        in_specs=[
            pl.BlockSpec((None, seq, ch), lambda b, g: (b, 0, first_lane_block + g)),
            pl.BlockSpec((n2, 2 * n1, n1), lambda b, g: (0, 0, 0)),
            pl.BlockSpec((2 * n2, 2 * n2), lambda b, g: (0, 0)),
            pl.BlockSpec((2 * ch, ch), lambda b, g: (0, 0)),
        ],
        out_specs=pl.BlockSpec((None, seq, ch), lambda b, g: (b, 0, g)),
        out_shape=jax.ShapeDtypeStruct((batch, seq, width), F32),
        scratch_shapes=[pltpu.VMEM((seq, ch), F32),
                        pltpu.VMEM((n1 * pitch, ch), F32),
                        pltpu.VMEM((n1 * pitch, ch), F32)],
        compiler_params=_cparams(("parallel", "parallel")),
        name="fourier",
    )(f3, m1, m2, csc)


def _out_proj_kernel(o1_ref, o4_ref, o16_ref, l1_ref, l4_ref, l16_ref, four_ref, x_ref,
                     ga_ref, gf_ref, wa_ref, wf_ref, g2_ref, e_ref, pt_ref, x1_ref, h2_ref,
                     ln4_ref, ln16_ref):
    tm = x_ref.shape[0]
    aw = o1_ref.shape[1]
    n_slabs = aw // LANES

    n4 = jnp.dot(pt_ref[0], o4_ref[...].reshape(tm, aw), preferred_element_type=F32)
    n16 = jnp.dot(pt_ref[1], o16_ref[...].reshape(tm, aw), preferred_element_type=F32)
    for r in range(4):
        ln4_ref[pl.ds(r, tm // 4, stride=4), :] = l4_ref[r]
    for r in range(16):
        ln16_ref[pl.ds(r, tm // 16, stride=16), :] = l16_ref[r]

    l1 = l1_ref[...]
    l2 = ln4_ref[...]
    l3 = ln16_ref[...]
    m = jnp.maximum(jnp.maximum(l1, l2), l3)
    e1 = jnp.exp(l1 - m)
    e2 = jnp.exp(l2 - m)
    e3 = jnp.exp(l3 - m)
    inv = 1.0 / (e1 + e2 + e3)
    expand = e_ref[...]

    def spread(w):
        hi = w.astype(BF16)
        lo = (w - hi.astype(F32)).astype(BF16)
        return jnp.dot(jnp.concatenate([hi, lo], axis=1), expand, preferred_element_type=F32)

    w1 = spread(e1 * inv)
    w2 = spread(e2 * inv)
    o1 = o1_ref[...].astype(F32)
    slabs = []
    ssq = jnp.zeros((tm, 1), F32)
    for c in range(n_slabs):
        lanes = slice(c * LANES, (c + 1) * LANES)
        o3 = n16[:, lanes]
        a = o3 + w1[:, lanes] * (o1[:, lanes] - o3) + w2[:, lanes] * (n4[:, lanes] - o3)
        ssq = ssq + jnp.sum(a * a, axis=-1, keepdims=True)
        slabs.append(a)
    attn = jnp.concatenate(slabs, axis=1)
    width = n_slabs * LANES
    a_n = (attn * lax.rsqrt(ssq / width + EPS) * ga_ref[...]).astype(BF16)
    f_n = _rms(four_ref[...], gf_ref[...]).astype(BF16)
    y = jnp.dot(a_n, wa_ref[...], preferred_element_type=F32) \
        + jnp.dot(f_n, wf_ref[...], preferred_element_type=F32)
    x1 = x_ref[...] + y
    x1_ref[...] = x1
    h2_ref[...] = _rms(x1, g2_ref[...]).astype(BF16)


def _out_proj(o1, o4, o16, l1, l4, l16, four, x2, ga, gf, wa, wf, g2, *, seq, tm=256):
    ntok, d_model = x2.shape
    aw = o1.shape[1]
    fw = four.shape[1]
    n_heads = aw // HEAD_DIM
    tps = seq // tm
    expand = np.zeros((LANES, aw), np.float32)
    for h in range(n_heads):
        expand[(h % 2) * HEAD_DIM + h // 2, h * HEAD_DIM:(h + 1) * HEAD_DIM] = 1.0
    expand = jnp.asarray(np.concatenate([expand, expand], axis=0)).astype(BF16)
    assert tm == PERM_ROWS
    perm = _class_permutation()
    perm_t = jnp.asarray(np.stack([perm[:PERM_ROWS].T, perm[PERM_ROWS:].T])).astype(BF16)
    row = lambda w: pl.BlockSpec((tm, w), lambda i: (i, 0))
    cls = lambda d, w: pl.BlockSpec((None, d, tm // d, w), lambda i: (i // tps, 0, i % tps, 0))
    const = lambda a, b: pl.BlockSpec((a, b), lambda i: (0, 0))
    return pl.pallas_call(
        _out_proj_kernel,
        grid=(ntok // tm,),
        in_specs=[row(aw), cls(4, aw), cls(16, aw), row(LANES), cls(4, LANES), cls(16, LANES),
                  row(fw), row(d_model),
                  const(1, aw), const(1, fw), const(aw, d_model), const(fw, d_model),
                  const(1, d_model), const(2 * LANES, aw),
                  pl.BlockSpec((2, tm, tm), lambda i: (0, 0, 0))],
        out_specs=[row(d_model), row(d_model)],
        out_shape=[jax.ShapeDtypeStruct((ntok, d_model), F32),
                   jax.ShapeDtypeStruct((ntok, d_model), BF16)],
        scratch_shapes=[pltpu.VMEM((tm, LANES), F32),
                        pltpu.VMEM((tm, LANES), F32)],
        compiler_params=_cparams(("parallel",)),
        name="out_proj",
    )(o1, o4, o16, l1, l4, l16, four, x2, ga, gf, wa, wf, g2, expand, perm_t)


HALO = 16


def _up_kernel(hp_ref, hc_ref, hn_ref, wg_ref, wv_ref, cwg_ref, cwv_ref, cbg_ref, cbv_ref,
               o_ref, lhs_ref, ug_ref, uv_ref, *, tm, tiles_per_seq):
    i = pl.program_id(0)
    j = pl.program_id(1)

    @pl.when(j == 0)
    def _():
        t = i % tiles_per_seq
        zero = jnp.zeros((HALO, lhs_ref.shape[1]), BF16)
        lhs_ref[0:HALO] = jnp.where(t == 0, zero, hp_ref[...])
        lhs_ref[HALO:HALO + tm] = hc_ref[...]
        lhs_ref[HALO + tm:HALO + tm + HALO] = jnp.where(t == tiles_per_seq - 1, zero, hn_ref[...])

    lhs = lhs_ref[...]
    ug_ref[...] = jnp.dot(lhs, wg_ref[...], preferred_element_type=F32)
    uv_ref[...] = jnp.dot(lhs, wv_ref[...], preferred_element_type=F32)

    def conv(u_ref, cw_ref, cb_ref):
        cw = cw_ref[...]
        return (u_ref[HALO - 1:HALO - 1 + tm] * cw[0:1]
                + u_ref[HALO:HALO + tm] * cw[1:2]
                + u_ref[HALO + 1:HALO + 1 + tm] * cw[2:3]
                + cb_ref[...])

    gate = conv(ug_ref, cwg_ref, cbg_ref)
    val = conv(uv_ref, cwv_ref, cbv_ref)
    o_ref[...] = (gate * (1.0 / (1.0 + jnp.exp(-gate))) * val).astype(BF16)


def _up_proj(h2, w_up, conv_w, conv_b, *, seq, tm=1024, tn=512):
    ntok, d_model = h2.shape
    d_ff = w_up.shape[1] // 2
    nj = d_ff // tn
    tps = seq // tm
    hb = tm // HALO
    last = ntok // HALO - 1
    kern = functools.partial(_up_kernel, tm=tm, tiles_per_seq=tps)
    return pl.pallas_call(
        kern,
        grid=(ntok // tm, nj),
        in_specs=[
            pl.BlockSpec((HALO, d_model), lambda i, j: (jnp.maximum(i * hb - 1, 0), 0)),
            pl.BlockSpec((tm, d_model), lambda i, j: (i, 0)),
            pl.BlockSpec((HALO, d_model), lambda i, j: (jnp.minimum((i + 1) * hb, last), 0)),
            pl.BlockSpec((d_model, tn), lambda i, j: (0, j)),
            pl.BlockSpec((d_model, tn), lambda i, j: (0, nj + j)),
            pl.BlockSpec((3, tn), lambda i, j: (0, j)),
            pl.BlockSpec((3, tn), lambda i, j: (0, nj + j)),
            pl.BlockSpec((1, tn), lambda i, j: (0, j)),
            pl.BlockSpec((1, tn), lambda i, j: (0, nj + j)),
        ],
        out_specs=pl.BlockSpec((tm, tn), lambda i, j: (i, j)),
        out_shape=jax.ShapeDtypeStruct((ntok, d_ff), BF16),
        scratch_shapes=[pltpu.VMEM((tm + 2 * HALO, d_model), BF16),
                        pltpu.VMEM((tm + 2 * HALO, tn), F32),
                        pltpu.VMEM((tm + 2 * HALO, tn), F32)],
        compiler_params=_cparams(("parallel", "arbitrary")),
        name="up_proj",
    )(h2, h2, h2, w_up, w_up, conv_w, conv_w, conv_b, conv_b)


def _down_kernel(a_ref, w_ref, x1_ref, g_ref, y_ref, *, tn, n_blocks):
    n = pl.program_id(1)
    y = jnp.dot(a_ref[...], w_ref[...], preferred_element_type=F32)
    for b in range(n_blocks):
        @pl.when(n == b)
        def _(b=b):
            cols = slice(b * tn, (b + 1) * tn)
            y_ref[:, cols] = x1_ref[...] + y

    @pl.when(n == n_blocks - 1)
    def _():
        y_ref[...] = _rms(y_ref[...], g_ref[...])


def _down_proj(act, w_down, x1, final_g, *, tm=1024, tn=256):
    ntok, d_ff = act.shape
    d_model = w_down.shape[1]
    n_blocks = d_model // tn
    kern = functools.partial(_down_kernel, tn=tn, n_blocks=n_blocks)
    return pl.pallas_call(
        kern,
        grid=(ntok // tm, n_blocks),
        in_specs=[
            pl.BlockSpec((tm, d_ff), lambda i, n: (i, 0)),
            pl.BlockSpec((d_ff, tn), lambda i, n: (0, n)),
            pl.BlockSpec((tm, tn), lambda i, n: (i, n)),
            pl.BlockSpec((1, d_model), lambda i, n: (0, 0)),
        ],
        out_specs=pl.BlockSpec((tm, d_model), lambda i, n: (i, 0)),
        out_shape=jax.ShapeDtypeStruct((ntok, d_model), F32),
        compiler_params=_cparams(("parallel", "arbitrary")),
        name="down_proj",
    )(act, w_down, x1, final_g)


def _rope_lane_tables(seq):
    inv_freq = ROPE_THETA ** (-jnp.arange(0, ROPE_DIM, 2, dtype=F32) / ROPE_DIM)
    ang = jnp.arange(seq, dtype=F32)[:, None] * inv_freq[None, :]
    cos, sin = jnp.cos(ang), jnp.sin(ang)
    half = ROPE_DIM // 2
    rest = HEAD_DIM - ROPE_DIM
    cos_h = jnp.concatenate([cos, cos, jnp.ones((seq, rest), F32)], axis=1)
    sa_h = jnp.concatenate([-sin, jnp.zeros((seq, HEAD_DIM - half), F32)], axis=1)
    sb_h = jnp.concatenate([jnp.zeros((seq, half), F32), sin, jnp.zeros((seq, rest), F32)], axis=1)
    reps = LANES // HEAD_DIM
    rope = jnp.stack([jnp.tile(t, (1, reps)) for t in (cos_h, sa_h, sb_h)])
    ident = jnp.stack([jnp.ones((seq, LANES), F32), jnp.zeros((seq, LANES), F32),
                       jnp.zeros((seq, LANES), F32)])
    return jnp.stack([rope, ident])


def _encoder(x, p):
    batch, seq, d_model = x.shape
    aw = p["attn_width"]
    ntok = batch * seq
    x2 = x.reshape(ntok, d_model)

    tn = N_FOURIER_GROUPS * FOURIER_GROUP
    nat, c4, c16 = _in_proj(x2, p["norm1_g"], p["w_in"], p["rope"], batch=batch, seq=seq,
                            n_rope_tiles=2 * aw // tn, tn=tn)
    nat3 = nat.reshape(batch, seq, -1)
    o1, l1 = _attention(nat3.reshape(batch, 1, seq, -1), attn_width=aw)
    o4, l4 = _attention(c4, attn_width=aw)
    o16, l16 = _attention(c16, attn_width=aw)
    four = _fourier(nat3, first_lane_block=3 * aw // FOURIER_GROUP)

    x1, h2 = _out_proj(o1.reshape(ntok, aw), o4, o16, l1.reshape(ntok, LANES), l4, l16,
                       four.reshape(ntok, -1), x2,
                       p["attn_out_g"], p["fourier_out_g"], p["w_out_a"], p["w_out_f"],
                       p["norm2_g"], seq=seq)
    act = _up_proj(h2, p["w_up"], p["conv_w"], p["conv_b"], seq=seq)
    y = _down_proj(act, p["w_down"], x1, p["final_g"])
    return y.reshape(batch, seq, d_model)


def _prepare(seq, norm1_g, w_in, attn_out_g, fourier_out_g, w_out, norm2_g, w_up, conv_w, conv_b,
             w_down, final_g):
    aw = (w_in.shape[1] - N_FOURIER_GROUPS * FOURIER_GROUP) // 3
    col_scale = jnp.where(jnp.arange(w_in.shape[1]) < aw, LOG2E / math.sqrt(HEAD_DIM), 1.0)
    w_out_b = w_out.astype(BF16)
    return {
        "attn_width": aw,
        "rope": _rope_lane_tables(seq),
        "norm1_g": norm1_g.reshape(1, -1),
        "w_in": (w_in * col_scale[None, :].astype(F32)).astype(BF16),
        "attn_out_g": attn_out_g.reshape(1, -1),
        "fourier_out_g": fourier_out_g.reshape(1, -1),
        "w_out_a": w_out_b[:aw],
        "w_out_f": w_out_b[aw:],
        "norm2_g": norm2_g.reshape(1, -1),
        "w_up": w_up.astype(BF16),
        "conv_w": conv_w,
        "conv_b": conv_b.reshape(1, -1),
        "w_down": w_down.astype(BF16),
        "final_g": final_g.reshape(1, -1),
    }


def kernel(x_prompt, x_sample, norm1_g, w_in, attn_out_g, fourier_out_g, w_out, norm2_g, w_up,
           conv_w, conv_b, w_down, final_g):
    assert norm1_g.shape[0] == 1, "single-layer encoder"
    assert x_prompt.shape[1] == x_sample.shape[1]
    p = _prepare(x_prompt.shape[1], norm1_g[0], w_in[0], attn_out_g[0], fourier_out_g[0], w_out[0],
                 norm2_g[0], w_up[0], conv_w[0], conv_b[0], w_down[0], final_g)
    return (_encoder(x_prompt, p), _encoder(x_sample, p))
```

```python
import functools
import math

import numpy as np
import jax
import jax.numpy as jnp
from jax import lax
from jax.experimental import pallas as pl
from jax.experimental.pallas import tpu as pltpu

F32 = jnp.float32
BF16 = jnp.bfloat16

HEAD_DIM = 64
ROPE_DIM = 16
ROPE_THETA = 500000.0
N_FOURIER_GROUPS = 4
FOURIER_GROUP = 128
DILATIONS = (1, 4, 16)
BAND = 64
EPS = 1e-6
MASK_VALUE = -1e30
LOG2E = 1.4426950408889634
LN2 = 0.6931471805599453

LANES = 128
VMEM_LIMIT_BYTES = 56 * 1024 * 1024

DFT_N1 = 64
DFT_PAD = 8


def _cparams(semantics):
    return pltpu.CompilerParams(dimension_semantics=semantics,
                                vmem_limit_bytes=VMEM_LIMIT_BYTES)


def _rms(x, g):
    var = jnp.mean(x * x, axis=-1, keepdims=True)
    return x * lax.rsqrt(var + EPS) * g


PERM_ROWS = 256


def _class_permutation():
    p = np.zeros((2 * PERM_ROWS, PERM_ROWS), np.float32)
    for d, base in ((4, 0), (16, PERM_ROWS)):
        per = PERM_ROWS // d
        for r in range(d):
            for i in range(per):
                p[base + r * per + i, d * i + r] = 1.0
    return p


def _in_proj_kernel(x_ref, g_ref, w_ref, tab_ref, perm_ref, nat_ref, c4_ref, c16_ref,
                    h_ref, acc_ref, *, nj):
    n = pl.program_id(0)
    slot = n % 2
    tm = acc_ref.shape[1]
    n_chunks = acc_ref.shape[2] // LANES

    @pl.when(n == 0)
    def _():
        acc_ref[1] = jnp.zeros(acc_ref.shape[1:], F32)

    @pl.when(n % nj == 0)
    def _():
        h_ref[...] = _rms(x_ref[...], g_ref[...]).astype(BF16)

    cos = tab_ref[0]
    sa = tab_ref[1]
    sb = tab_ref[2]
    for c in range(n_chunks):
        lanes = slice(c * LANES, (c + 1) * LANES)
        a = acc_ref[1 - slot, :, lanes]
        r = a * cos + pltpu.roll(a, LANES - ROPE_DIM // 2, 1) * sa \
            + pltpu.roll(a, ROPE_DIM // 2, 1) * sb
        nat_ref[:, lanes] = r.astype(BF16)

    acc_ref[slot] = jnp.dot(h_ref[...], w_ref[...], preferred_element_type=F32)

    perm = perm_ref[...]
    per4 = PERM_ROWS // 4
    per16 = PERM_ROWS // 16
    for s in range(tm // PERM_ROWS):
        slab = nat_ref[s * PERM_ROWS:(s + 1) * PERM_ROWS, :]
        res = jnp.dot(perm, slab, preferred_element_type=F32).astype(BF16)
        for r4 in range(4):
            c4_ref[r4, s * per4:(s + 1) * per4, :] = res[r4 * per4:(r4 + 1) * per4]
        for r16 in range(16):
            c16_ref[r16, s * per16:(s + 1) * per16, :] = \
                res[PERM_ROWS + r16 * per16:PERM_ROWS + (r16 + 1) * per16]


def _in_proj(x2, g, w, tabs, *, batch, seq, n_rope_tiles, tm=1024, tn=512):
    ntok, d_model = x2.shape
    width = w.shape[1]
    nj = width // tn
    ntiles = ntok // tm
    assert tm % PERM_ROWS == 0
    tps = seq // tm
    kern = functools.partial(_in_proj_kernel, nj=nj)
    perm = jnp.asarray(_class_permutation()).astype(BF16)

    def prev(n):
        pn = jnp.maximum(n - 1, 0)
        return pn // nj, pn % nj

    def tab_map(n):
        pi, pj = prev(n)
        return (jnp.where(pj < n_rope_tiles, 0, 1), 0, pi % tps, 0)

    def nat_map(n):
        return prev(n)

    def cls_map(n):
        pi, pj = prev(n)
        return (pi // tps, 0, pi % tps, pj)

    return pl.pallas_call(
        kern,
        grid=(ntiles * nj + 1,),
        in_specs=[
            pl.BlockSpec((tm, d_model), lambda n: (jnp.minimum(n // nj, ntiles - 1), 0)),
            pl.BlockSpec((1, d_model), lambda n: (0, 0)),
            pl.BlockSpec((d_model, tn), lambda n: (0, n % nj)),
            pl.BlockSpec((None, 3, tm, LANES), tab_map),
            pl.BlockSpec((2 * PERM_ROWS, PERM_ROWS), lambda n: (0, 0)),
        ],
        out_specs=[
            pl.BlockSpec((tm, tn), nat_map),
            pl.BlockSpec((None, 4, tm // 4, tn), cls_map),
            pl.BlockSpec((None, 16, tm // 16, tn), cls_map),
        ],
        out_shape=[
            jax.ShapeDtypeStruct((ntok, width), BF16),
            jax.ShapeDtypeStruct((batch, 4, seq // 4, width), BF16),
            jax.ShapeDtypeStruct((batch, 16, seq // 16, width), BF16),
        ],
        scratch_shapes=[pltpu.VMEM((tm, d_model), BF16),
                        pltpu.VMEM((2, tm, tn), F32)],
        compiler_params=_cparams(("arbitrary",)),
        name="in_proj",
    )(x2, g, w, tabs, perm)


def _attn_kernel(bias_ref, q_ref, kp_ref, kc_ref, kn_ref, vp_ref, vc_ref, vn_ref,
                 o_ref, lse_ref, p_ref, m_ref, d_ref,
                 *, tb, tq, n_chunks, pairs, row_chunk=16):
    j = pl.program_id(2)
    tk = tq + 2 * BAND
    n_sub = tb // tq

    lane = lax.broadcasted_iota(jnp.int32, (tq, LANES), 1)
    first_head = lane < HEAD_DIM
    first_chunk = lax.broadcasted_iota(jnp.int32, (row_chunk, LANES), 1) < HEAD_DIM

    def window(prev_ref, cur_ref, next_ref, i, lanes):
        if i == 0:
            return jnp.concatenate([prev_ref[:, lanes], cur_ref[0:tq + BAND, lanes]], axis=0)
        if i == n_sub - 1:
            return jnp.concatenate([cur_ref[tb - tq - BAND:tb, lanes], next_ref[:, lanes]], axis=0)
        return cur_ref[i * tq - BAND:(i + 1) * tq + BAND, lanes]

    units = [(i, p) for i in range(n_sub) for p in range(pairs)]

    def stage_a(u):
        i, p = units[u]
        slot = u % 2
        lanes = slice(p * LANES, (p + 1) * LANES)
        q = q_ref[i * tq:(i + 1) * tq, lanes]
        zero = jnp.zeros_like(q)
        q2 = jnp.concatenate([jnp.where(first_head, q, zero), jnp.where(first_head, zero, q)],
                             axis=0)
        kw = window(kp_ref, kc_ref, kn_ref, i, lanes)
        s = lax.dot_general(q2, kw, (((1,), (1,)), ((), ())), preferred_element_type=F32)
        if i == 0:
            idx = jnp.where(j == 0, 1, 0)
        elif i == n_sub - 1:
            idx = jnp.where(j == n_chunks - 1, 2, 0)
        else:
            idx = 0
        s = s + bias_ref[idx]
        for r in range(0, tq, row_chunk):
            s0 = s[r:r + row_chunk]
            s1 = s[tq + r:tq + r + row_chunk]
            m0 = jnp.max(s0, axis=-1, keepdims=True)
            m1 = jnp.max(s1, axis=-1, keepdims=True)
            e0 = jnp.exp2(s0 - m0)
            e1 = jnp.exp2(s1 - m1)
            p_ref[slot, r:r + row_chunk, :] = e0.astype(BF16)
            p_ref[slot, tq + r:tq + r + row_chunk, :] = e1.astype(BF16)
            m_ref[slot, r:r + row_chunk, :] = jnp.where(
                first_chunk, jnp.broadcast_to(m0, (row_chunk, LANES)),
                jnp.broadcast_to(m1, (row_chunk, LANES)))
            d0 = jnp.sum(e0, axis=-1, keepdims=True)
            d1 = jnp.sum(e1, axis=-1, keepdims=True)
            d_ref[slot, r:r + row_chunk, :] = jnp.where(
                first_chunk, jnp.broadcast_to(d0, (row_chunk, LANES)),
                jnp.broadcast_to(d1, (row_chunk, LANES)))

    def stage_c(u):
        i, p = units[u]
        slot = u % 2
        lanes = slice(p * LANES, (p + 1) * LANES)
        pv = jnp.dot(p_ref[slot], window(vp_ref, vc_ref, vn_ref, i, lanes),
                     preferred_element_type=F32)
        o = jnp.where(first_head, pv[:tq], pv[tq:])
        dens = d_ref[slot]
        o_ref[i * tq:(i + 1) * tq, lanes] = (o * (1.0 / dens)).astype(BF16)
        lse = m_ref[slot] * LN2 + jnp.log(dens)
        cur = jnp.zeros((tq, LANES), F32) if p == 0 else lse_ref[i * tq:(i + 1) * tq, :]
        lse_ref[i * tq:(i + 1) * tq, :] = jnp.where(
            jnp.bitwise_and(lane, HEAD_DIM - 1) == p, lse, cur)

    n_units = len(units)
    for t in range(n_units + 1):
        if t < n_units:
            stage_a(t)
        if 0 <= t - 1 < n_units:
            stage_c(t - 1)


def _attn_bias(tq):
    tk = tq + 2 * BAND
    col = np.arange(tk)[None, :]
    row = np.arange(tq)[:, None]
    band = (col - row >= 0) & (col - row <= 2 * BAND)
    variants = [band, band & (col >= BAND), band & (col < tq + BAND)]
    bias = np.stack([np.where(v, 0.0, MASK_VALUE) for v in variants]).astype(np.float32)
    return jnp.asarray(np.concatenate([bias, bias], axis=1))


def _attention(qkv, *, attn_width, tq=128, max_tb=512):
    batch, d, t_len, width = qkv.shape
    tb = min(max_tb, t_len)
    assert t_len % tb == 0 and tb % tq == 0 and tb // tq >= 2
    n_chunks = t_len // tb
    pairs = attn_width // LANES
    hb = tb // BAND
    last_hb = t_len // BAND - 1
    tk = tq + 2 * BAND

    def cur(off):
        return pl.BlockSpec((None, None, tb, attn_width), lambda b, r, j: (b, r, j, off))

    def prv(off):
        return pl.BlockSpec((None, None, BAND, attn_width),
                            lambda b, r, j: (b, r, jnp.maximum(j * hb - 1, 0), off))

    def nxt(off):
        return pl.BlockSpec((None, None, BAND, attn_width),
                            lambda b, r, j: (b, r, jnp.minimum((j + 1) * hb, last_hb), off))

    kern = functools.partial(_attn_kernel, tb=tb, tq=tq, n_chunks=n_chunks, pairs=pairs)
    return pl.pallas_call(
        kern,
        grid=(batch, d, n_chunks),
        in_specs=[pl.BlockSpec((3, 2 * tq, tk), lambda b, r, j: (0, 0, 0)),
                  cur(0), prv(1), cur(1), nxt(1), prv(2), cur(2), nxt(2)],
        out_specs=[
            pl.BlockSpec((None, None, tb, attn_width), lambda b, r, j: (b, r, j, 0)),
            pl.BlockSpec((None, None, tb, LANES), lambda b, r, j: (b, r, j, 0)),
        ],
        out_shape=[
            jax.ShapeDtypeStruct((batch, d, t_len, attn_width), BF16),
            jax.ShapeDtypeStruct((batch, d, t_len, LANES), F32),
        ],
        scratch_shapes=[pltpu.VMEM((2, 2 * tq, tk), BF16),
                        pltpu.VMEM((2, tq, LANES), F32),
                        pltpu.VMEM((2, tq, LANES), F32)],
        compiler_params=_cparams(("parallel", "parallel", "parallel")),
        name=f"attn_d{d}",
    )(_attn_bias(tq), qkv, qkv, qkv, qkv, qkv, qkv, qkv)


def _fourier_kernel(x_ref, m1_ref, m2_ref, csc_ref, o_ref, xf_ref, yr_ref, yi_ref,
                    *, n1, n2, pitch, scale):
    xf_ref[...] = x_ref[...].astype(F32)

    def stage1(m, carry):
        xm = xf_ref[pl.ds(m, n1, stride=n2), :].astype(BF16)
        y = jnp.dot(m1_ref[m], xm, preferred_element_type=F32)
        yr_ref[pl.ds(m, n1, stride=pitch), :] = y[:n1]
        yi_ref[pl.ds(m, n1, stride=pitch), :] = y[n1:]
        return carry

    lax.fori_loop(0, n2, stage1, 0, unroll=8)

    m2 = m2_ref[...]
    csc = csc_ref[...]

    def stage2(kk, carry):
        ka = 2 * kk
        kb = ka + 1
        base_a = pl.multiple_of(ka * pitch, 8)
        base_b = pl.multiple_of(kb * pitch, 8)
        dat_a = jnp.concatenate([yr_ref[pl.ds(base_a, n2), :], yi_ref[pl.ds(base_a, n2), :]], axis=0)
        dat_b = jnp.concatenate([yr_ref[pl.ds(base_b, n2), :], yi_ref[pl.ds(base_b, n2), :]], axis=0)
        dat = jnp.concatenate([dat_a, dat_b], axis=1).astype(BF16)
        o = jnp.dot(m2, dat, preferred_element_type=F32).astype(BF16)
        lhs = jnp.concatenate(
            [jnp.concatenate([o[:n2, :LANES], o[n2:, :LANES]], axis=1),
             jnp.concatenate([o[:n2, LANES:], o[n2:, LANES:]], axis=1)], axis=0)
        res = jnp.dot(lhs, csc, preferred_element_type=F32) * scale
        o_ref[pl.ds(ka, n2, stride=n1), :] = res[:n2]
        o_ref[pl.ds(kb, n2, stride=n1), :] = res[n2:]
        return carry

    lax.fori_loop(0, n1 // 2, stage2, 0, unroll=4)


def _dft_mats(n):
    k = np.arange(n)
    ang = 2.0 * np.pi * ((k[:, None] * k[None, :]) % n) / n
    return np.cos(ang), np.sin(ang)


def _fourier(f3, *, first_lane_block=0):
    batch, seq, _ = f3.shape
    ch = FOURIER_GROUP
    groups = N_FOURIER_GROUPS
    width = groups * ch
    n1 = DFT_N1
    n2 = seq // n1
    pitch = n2 + DFT_PAD
    pos = n2 * np.arange(n1)[None, None, :] + np.arange(n2)[:, None, None]
    ang = 2.0 * np.pi * ((np.arange(n1)[None, :, None] * pos) % seq) / seq
    m1 = jnp.asarray(np.concatenate([np.cos(ang), -np.sin(ang)], axis=1), F32).astype(BF16)
    c2, s2 = _dft_mats(n2)
    m2 = jnp.asarray(np.block([[c2, s2], [-s2, c2]]), F32).astype(BF16)
    cch, sch = _dft_mats(ch)
    csc = jnp.asarray(np.concatenate([cch, sch], axis=0), F32).astype(BF16)
    assert n1 % 2 == 0
    kern = functools.partial(_fourier_kernel, n1=n1, n2=n2, pitch=pitch,
                             scale=1.0 / math.sqrt(seq * ch))
    return pl.pallas_call(
        kern,
        grid=(batch, groups),
        in_specs=[
            pl.BlockSpec((None, seq, ch), lambda b, g: (b, 0, first_lane_block + g)),
            pl.BlockSpec((n2, 2 * n1, n1), lambda b, g: (0, 0, 0)),
            pl.BlockSpec((2 * n2, 2 * n2), lambda b, g: (0, 0)),
            pl.BlockSpec((2 * ch, ch), lambda b, g: (0, 0)),
        ],
        out_specs=pl.BlockSpec((None, seq, ch), lambda b, g: (b, 0, g)),
        out_shape=jax.ShapeDtypeStruct((batch, seq, width), F32),
        scratch_shapes=[pltpu.VMEM((seq, ch), F32),
                        pltpu.VMEM((n1 * pitch, ch), F32),
                        pltpu.VMEM((n1 * pitch, ch), F32)],
        compiler_params=_cparams(("parallel", "parallel")),
        name="fourier",
    )(f3, m1, m2, csc)


def _out_proj_kernel(o1_ref, o4_ref, o16_ref, l1_ref, l4_ref, l16_ref, four_ref, x_ref,
                     ga_ref, gf_ref, wa_ref, wf_ref, g2_ref, e_ref, pt_ref, x1_ref, h2_ref,
                     ln4_ref, ln16_ref):
    tm = x_ref.shape[0]
    aw = o1_ref.shape[1]
    n_slabs = aw // LANES

    n4 = jnp.dot(pt_ref[0], o4_ref[...].reshape(tm, aw), preferred_element_type=F32)
    n16 = jnp.dot(pt_ref[1], o16_ref[...].reshape(tm, aw), preferred_element_type=F32)
    for r in range(4):
        ln4_ref[pl.ds(r, tm // 4, stride=4), :] = l4_ref[r]
    for r in range(16):
        ln16_ref[pl.ds(r, tm // 16, stride=16), :] = l16_ref[r]

    l1 = l1_ref[...]
    l2 = ln4_ref[...]
    l3 = ln16_ref[...]
    m = jnp.maximum(jnp.maximum(l1, l2), l3)
    e1 = jnp.exp(l1 - m)
    e2 = jnp.exp(l2 - m)
    e3 = jnp.exp(l3 - m)
    inv = 1.0 / (e1 + e2 + e3)
    expand = e_ref[...]

    def spread(w):
        hi = w.astype(BF16)
        lo = (w - hi.astype(F32)).astype(BF16)
        return jnp.dot(jnp.concatenate([hi, lo], axis=1), expand, preferred_element_type=F32)

    w1 = spread(e1 * inv)
    w2 = spread(e2 * inv)
    o1 = o1_ref[...].astype(F32)
    slabs = []
    ssq = jnp.zeros((tm, 1), F32)
    for c in range(n_slabs):
        lanes = slice(c * LANES, (c + 1) * LANES)
        o3 = n16[:, lanes]
        a = o3 + w1[:, lanes] * (o1[:, lanes] - o3) + w2[:, lanes] * (n4[:, lanes] - o3)
        ssq = ssq + jnp.sum(a * a, axis=-1, keepdims=True)
        slabs.append(a)
    attn = jnp.concatenate(slabs, axis=1)
    width = n_slabs * LANES
    a_n = (attn * lax.rsqrt(ssq / width + EPS) * ga_ref[...]).astype(BF16)
    f_n = _rms(four_ref[...], gf_ref[...]).astype(BF16)
    y = jnp.dot(a_n, wa_ref[...], preferred_element_type=F32) \
        + jnp.dot(f_n, wf_ref[...], preferred_element_type=F32)
    x1 = x_ref[...] + y
    x1_ref[...] = x1
    h2_ref[...] = _rms(x1, g2_ref[...]).astype(BF16)


def _out_proj(o1, o4, o16, l1, l4, l16, four, x2, ga, gf, wa, wf, g2, *, seq, tm=256):
    ntok, d_model = x2.shape
    aw = o1.shape[1]
    fw = four.shape[1]
    n_heads = aw // HEAD_DIM
    tps = seq // tm
    expand = np.zeros((LANES, aw), np.float32)
    for h in range(n_heads):
        expand[(h % 2) * HEAD_DIM + h // 2, h * HEAD_DIM:(h + 1) * HEAD_DIM] = 1.0
    expand = jnp.asarray(np.concatenate([expand, expand], axis=0)).astype(BF16)
    assert tm == PERM_ROWS
    perm = _class_permutation()
    perm_t = jnp.asarray(np.stack([perm[:PERM_ROWS].T, perm[PERM_ROWS:].T])).astype(BF16)
    row = lambda w: pl.BlockSpec((tm, w), lambda i: (i, 0))
    cls = lambda d, w: pl.BlockSpec((None, d, tm // d, w), lambda i: (i // tps, 0, i % tps, 0))
    const = lambda a, b: pl.BlockSpec((a, b), lambda i: (0, 0))
    return pl.pallas_call(
        _out_proj_kernel,
        grid=(ntok // tm,),
        in_specs=[row(aw), cls(4, aw), cls(16, aw), row(LANES), cls(4, LANES), cls(16, LANES),
                  row(fw), row(d_model),
                  const(1, aw), const(1, fw), const(aw, d_model), const(fw, d_model),
                  const(1, d_model), const(2 * LANES, aw),
                  pl.BlockSpec((2, tm, tm), lambda i: (0, 0, 0))],
        out_specs=[row(d_model), row(d_model)],
        out_shape=[jax.ShapeDtypeStruct((ntok, d_model), F32),
                   jax.ShapeDtypeStruct((ntok, d_model), BF16)],
        scratch_shapes=[pltpu.VMEM((tm, LANES), F32),
                        pltpu.VMEM((tm, LANES), F32)],
        compiler_params=_cparams(("parallel",)),
        name="out_proj",
    )(o1, o4, o16, l1, l4, l16, four, x2, ga, gf, wa, wf, g2, expand, perm_t)


HALO = 16


def _up_kernel(hp_ref, hc_ref, hn_ref, wg_ref, wv_ref, cwg_ref, cwv_ref, cbg_ref, cbv_ref,
               o_ref, lhs_ref, ug_ref, uv_ref, *, tm, tiles_per_seq):
    i = pl.program_id(0)
    j = pl.program_id(1)

    @pl.when(j == 0)
    def _():
        t = i % tiles_per_seq
        zero = jnp.zeros((HALO, lhs_ref.shape[1]), BF16)
        lhs_ref[0:HALO] = jnp.where(t == 0, zero, hp_ref[...])
        lhs_ref[HALO:HALO + tm] = hc_ref[...]
        lhs_ref[HALO + tm:HALO + tm + HALO] = jnp.where(t == tiles_per_seq - 1, zero, hn_ref[...])

    lhs = lhs_ref[...]
    ug_ref[...] = jnp.dot(lhs, wg_ref[...], preferred_element_type=F32)
    uv_ref[...] = jnp.dot(lhs, wv_ref[...], preferred_element_type=F32)

    def conv(u_ref, cw_ref, cb_ref):
        cw = cw_ref[...]
        return (u_ref[HALO - 1:HALO - 1 + tm] * cw[0:1]
                + u_ref[HALO:HALO + tm] * cw[1:2]
                + u_ref[HALO + 1:HALO + 1 + tm] * cw[2:3]
                + cb_ref[...])

    gate = conv(ug_ref, cwg_ref, cbg_ref)
    val = conv(uv_ref, cwv_ref, cbv_ref)
    o_ref[...] = (gate * (1.0 / (1.0 + jnp.exp(-gate))) * val).astype(BF16)


def _up_proj(h2, w_up, conv_w, conv_b, *, seq, tm=1024, tn=512):
    ntok, d_model = h2.shape
    d_ff = w_up.shape[1] // 2
    nj = d_ff // tn
    tps = seq // tm
    hb = tm // HALO
    last = ntok // HALO - 1
    kern = functools.partial(_up_kernel, tm=tm, tiles_per_seq=tps)
    return pl.pallas_call(
        kern,
        grid=(ntok // tm, nj),
        in_specs=[
            pl.BlockSpec((HALO, d_model), lambda i, j: (jnp.maximum(i * hb - 1, 0), 0)),
            pl.BlockSpec((tm, d_model), lambda i, j: (i, 0)),
            pl.BlockSpec((HALO, d_model), lambda i, j: (jnp.minimum((i + 1) * hb, last), 0)),
            pl.BlockSpec((d_model, tn), lambda i, j: (0, j)),
            pl.BlockSpec((d_model, tn), lambda i, j: (0, nj + j)),
            pl.BlockSpec((3, tn), lambda i, j: (0, j)),
            pl.BlockSpec((3, tn), lambda i, j: (0, nj + j)),
            pl.BlockSpec((1, tn), lambda i, j: (0, j)),
            pl.BlockSpec((1, tn), lambda i, j: (0, nj + j)),
        ],
        out_specs=pl.BlockSpec((tm, tn), lambda i, j: (i, j)),
        out_shape=jax.ShapeDtypeStruct((ntok, d_ff), BF16),
        scratch_shapes=[pltpu.VMEM((tm + 2 * HALO, d_model), BF16),
                        pltpu.VMEM((tm + 2 * HALO, tn), F32),
                        pltpu.VMEM((tm + 2 * HALO, tn), F32)],
        compiler_params=_cparams(("parallel", "arbitrary")),
        name="up_proj",
    )(h2, h2, h2, w_up, w_up, conv_w, conv_w, conv_b, conv_b)


def _down_kernel(a_ref, w_ref, x1_ref, g_ref, y_ref, *, tn, n_blocks):
    n = pl.program_id(1)
    y = jnp.dot(a_ref[...], w_ref[...], preferred_element_type=F32)
    for b in range(n_blocks):
        @pl.when(n == b)
        def _(b=b):
            cols = slice(b * tn, (b + 1) * tn)
            y_ref[:, cols] = x1_ref[...] + y

    @pl.when(n == n_blocks - 1)
    def _():
        y_ref[...] = _rms(y_ref[...], g_ref[...])


def _down_proj(act, w_down, x1, final_g, *, tm=1024, tn=256):
    ntok, d_ff = act.shape
    d_model = w_down.shape[1]
    n_blocks = d_model // tn
    kern = functools.partial(_down_kernel, tn=tn, n_blocks=n_blocks)
    return pl.pallas_call(
        kern,
        grid=(ntok // tm, n_blocks),
        in_specs=[
            pl.BlockSpec((tm, d_ff), lambda i, n: (i, 0)),
            pl.BlockSpec((d_ff, tn), lambda i, n: (0, n)),
            pl.BlockSpec((tm, tn), lambda i, n: (i, n)),
            pl.BlockSpec((1, d_model), lambda i, n: (0, 0)),
        ],
        out_specs=pl.BlockSpec((tm, d_model), lambda i, n: (i, 0)),
        out_shape=jax.ShapeDtypeStruct((ntok, d_model), F32),
        compiler_params=_cparams(("parallel", "arbitrary")),
        name="down_proj",
    )(act, w_down, x1, final_g)


def _rope_lane_tables(seq):
    inv_freq = ROPE_THETA ** (-jnp.arange(0, ROPE_DIM, 2, dtype=F32) / ROPE_DIM)
    ang = jnp.arange(seq, dtype=F32)[:, None] * inv_freq[None, :]
    cos, sin = jnp.cos(ang), jnp.sin(ang)
    half = ROPE_DIM // 2
    rest = HEAD_DIM - ROPE_DIM
    cos_h = jnp.concatenate([cos, cos, jnp.ones((seq, rest), F32)], axis=1)
    sa_h = jnp.concatenate([-sin, jnp.zeros((seq, HEAD_DIM - half), F32)], axis=1)
    sb_h = jnp.concatenate([jnp.zeros((seq, half), F32), sin, jnp.zeros((seq, rest), F32)], axis=1)
    reps = LANES // HEAD_DIM
    rope = jnp.stack([jnp.tile(t, (1, reps)) for t in (cos_h, sa_h, sb_h)])
    ident = jnp.stack([jnp.ones((seq, LANES), F32), jnp.zeros((seq, LANES), F32),
                       jnp.zeros((seq, LANES), F32)])
    return jnp.stack([rope, ident])


def _encoder(x, p):
    batch, seq, d_model = x.shape
    aw = p["attn_width"]
    ntok = batch * seq
    x2 = x.reshape(ntok, d_model)

    tn = N_FOURIER_GROUPS * FOURIER_GROUP
    nat, c4, c16 = _in_proj(x2, p["norm1_g"], p["w_in"], p["rope"], batch=batch, seq=seq,
                            n_rope_tiles=2 * aw // tn, tn=tn)
    nat3 = nat.reshape(batch, seq, -1)
    o1, l1 = _attention(nat3.reshape(batch, 1, seq, -1), attn_width=aw)
    o4, l4 = _attention(c4, attn_width=aw)
    o16, l16 = _attention(c16, attn_width=aw)
    four = _fourier(nat3, first_lane_block=3 * aw // FOURIER_GROUP)

    x1, h2 = _out_proj(o1.reshape(ntok, aw), o4, o16, l1.reshape(ntok, LANES), l4, l16,
                       four.reshape(ntok, -1), x2,
                       p["attn_out_g"], p["fourier_out_g"], p["w_out_a"], p["w_out_f"],
                       p["norm2_g"], seq=seq)
    act = _up_proj(h2, p["w_up"], p["conv_w"], p["conv_b"], seq=seq)
    y = _down_proj(act, p["w_down"], x1, p["final_g"])
    return y.reshape(batch, seq, d_model)


def _prepare(seq, norm1_g, w_in, attn_out_g, fourier_out_g, w_out, norm2_g, w_up, conv_w, conv_b,
             w_down, final_g):
    aw = (w_in.shape[1] - N_FOURIER_GROUPS * FOURIER_GROUP) // 3
    col_scale = jnp.where(jnp.arange(w_in.shape[1]) < aw, LOG2E / math.sqrt(HEAD_DIM), 1.0)
    w_out_b = w_out.astype(BF16)
    return {
        "attn_width": aw,
        "rope": _rope_lane_tables(seq),
        "norm1_g": norm1_g.reshape(1, -1),
        "w_in": (w_in * col_scale[None, :].astype(F32)).astype(BF16),
        "attn_out_g": attn_out_g.reshape(1, -1),
        "fourier_out_g": fourier_out_g.reshape(1, -1),
        "w_out_a": w_out_b[:aw],
        "w_out_f": w_out_b[aw:],
        "norm2_g": norm2_g.reshape(1, -1),
        "w_up": w_up.astype(BF16),
        "conv_w": conv_w,
        "conv_b": conv_b.reshape(1, -1),
        "w_down": w_down.astype(BF16),
        "final_g": final_g.reshape(1, -1),
    }


def kernel(x_prompt, x_sample, norm1_g, w_in, attn_out_g, fourier_out_g, w_out, norm2_g, w_up,
           conv_w, conv_b, w_down, final_g):
    assert norm1_g.shape[0] == 1, "single-layer encoder"
    assert x_prompt.shape[1] == x_sample.shape[1]
    p = _prepare(x_prompt.shape[1], norm1_g[0], w_in[0], attn_out_g[0], fourier_out_g[0], w_out[0],
                 norm2_g[0], w_up[0], conv_w[0], conv_b[0], w_down[0], final_g)
    return (_encoder(x_prompt, p), _encoder(x_sample, p))
```
